```python
import math
import jax, jax.numpy as jnp
from jax import lax
import numpy as np

D_MODEL = 1024
BATCH = 8
SEQ = 2048
DEPTH = 4
DEC_BATCH = 128
DEC_SEQ = 8
PAST_LEN = 16384
PAGE_SIZE = 128

H_A = 8
DK_A = 128
DV_A = 128
QK_A = H_A * DK_A
D_A = H_A * DV_A
CHUNK = 64
D_B = 1024
CONV_W = 31
PLE_DIM = 256
EPS = 1e-6
SPLIT_SIZES = (QK_A, QK_A, D_A, D_A, D_B, D_B, D_B, D_MODEL, D_MODEL)
N_IN = sum(SPLIT_SIZES)
SPLIT_POINTS = tuple(int(v) for v in np.cumsum(SPLIT_SIZES)[:-1])

kernel_name = "hybrid_hgrn2_conformer_step"


def _rmsnorm(x, g):
    xf = x.astype(jnp.float32)
    y = xf * lax.rsqrt(jnp.mean(xf * xf, axis=-1, keepdims=True) + EPS)
    return (y * g.astype(jnp.float32)).astype(x.dtype)


def _layernorm(x, g, b):
    xf = x.astype(jnp.float32)
    mu = jnp.mean(xf, axis=-1, keepdims=True)
    var = jnp.mean(jnp.square(xf - mu), axis=-1, keepdims=True)
    y = (xf - mu) * lax.rsqrt(var + EPS)
    return (y * g.astype(jnp.float32) + b.astype(jnp.float32)).astype(x.dtype)


def _hgrn2_chunked(q, f, v, s0):
    B, T, H, K = q.shape
    V = v.shape[-1]
    c = min(CHUNK, T)
    n = -(-T // c)
    pad = n * c - T
    k = 1.0 - f
    log_f = jnp.log(jnp.maximum(f, 1e-30))
    if pad:
        pw = ((0, 0), (0, pad), (0, 0), (0, 0))
        q, log_f, k, v = (jnp.pad(a, pw) for a in (q, log_f, k, v))

    def to_chunks(a):
        return a.reshape(B, n, c, H, a.shape[-1]).transpose(1, 0, 3, 2, 4)

    mask = jnp.tril(jnp.ones((c, c), dtype=bool))[:, :, None]

    def step(S, inp):
        qc, lfc, kc, vc = inp
        b = jnp.cumsum(lfc, axis=-2)
        diff = b[..., :, None, :] - b[..., None, :, :]
        decay = jnp.exp(jnp.where(mask, diff, -jnp.inf))
        attn = jnp.einsum('bhtk,bhsk,bhtsk->bhts', qc, kc, decay)
        o = (jnp.einsum('bhts,bhsv->bhtv', attn, vc)
             + jnp.einsum('bhtk,bhkv->bhtv', qc * jnp.exp(b), S))
        b_last = b[..., -1:, :]
        S_new = (jnp.exp(b_last[..., 0, :])[..., None] * S
                 + jnp.einsum('bhsk,bhsv->bhkv', kc * jnp.exp(b_last - b), vc))
        return S_new, o

    S, o = lax.scan(step, s0, (to_chunks(q), to_chunks(log_f), to_chunks(k), to_chunks(v)))
    o = o.transpose(1, 0, 3, 2, 4).reshape(B, n * c, H, V)[:, :T]
    return o, S


def _layer(h, p_l, s0, conv_buf, lb, norm_mix, w_in, b_in, gnorm_a, w_br_a, conv_w, conv_b,
           ln_g, ln_b, w_br_b, w_o, w_ple, w_ple_gate, norm_ple):
    B, T, _ = h.shape
    xn = _rmsnorm(h, norm_mix)
    proj = xn @ w_in + b_in
    q, fz, iv, za, ga, gb, zb, ma, mb = jnp.split(proj, SPLIT_POINTS, axis=-1)

    f = lb + (1.0 - lb) * jax.nn.sigmoid(fz.astype(jnp.float32))
    qf = jax.nn.silu(q.astype(jnp.float32)).reshape(B, T, H_A, DK_A)
    o, s_new = _hgrn2_chunked(qf, f.reshape(B, T, H_A, DK_A),
                              iv.astype(jnp.float32).reshape(B, T, H_A, DV_A),
                              s0.astype(jnp.float32))
    o = _rmsnorm(o, gnorm_a.reshape(H_A, DV_A)).reshape(B, T, D_A).astype(h.dtype)
    y_a = (o * jax.nn.silu(za)) @ w_br_a

    u = ga * jax.nn.sigmoid(gb)
    u_full = jnp.concatenate([conv_buf.astype(u.dtype), u], axis=1)
    cv = lax.conv_general_dilated(u_full, conv_w[:, None, :], window_strides=(1,), padding='VALID',
                                  dimension_numbers=('NWC', 'WIO', 'NWC'),
                                  feature_group_count=D_B) + conv_b
    cv = jax.nn.silu(_layernorm(cv, ln_g, ln_b))
    y_b = (cv * jax.nn.silu(zb)) @ w_br_b

    merged = jax.nn.sigmoid(ma) * y_a + jax.nn.sigmoid(mb) * y_b
    h = h + merged @ w_o

    gate = jax.nn.sigmoid(_rmsnorm(h, norm_ple) @ w_ple_gate)
    h = h + gate * (p_l @ w_ple)
    return h, s_new.astype(h.dtype), u_full[:, -(CONV_W - 1):]


def setup_inputs(seed: int = 0) -> dict:
    key = jax.random.key(seed)
    ks = jax.random.split(key, 24)
    nrm = jax.random.normal
    f32 = jnp.float32
    return {
        "x_prompt": nrm(ks[0], (BATCH, SEQ, D_MODEL), f32),
        "x_sample": nrm(ks[1], (DEC_BATCH, DEC_SEQ, D_MODEL), f32),
        "state_hgrn": 0.5 * nrm(ks[2], (DEPTH, DEC_BATCH, H_A, DK_A, DV_A), f32),
        "state_conv": 0.5 * nrm(ks[3], (DEPTH, DEC_BATCH, CONV_W - 1, D_B), f32),
        "p_prompt": nrm(ks[4], (DEPTH, BATCH, SEQ, PLE_DIM), f32),
        "p_sample": nrm(ks[5], (DEPTH, DEC_BATCH, DEC_SEQ, PLE_DIM), f32),
        "lb_param": nrm(ks[6], (DEPTH, QK_A), f32),
        "norm_mix": 1.0 + 0.01 * nrm(ks[7], (DEPTH, D_MODEL), f32),
        "w_in": nrm(ks[8], (DEPTH, D_MODEL, N_IN), f32) * D_MODEL ** -0.5,
        "b_in": 0.01 * nrm(ks[9], (DEPTH, N_IN), f32),
        "gnorm_a": 1.0 + 0.01 * nrm(ks[10], (DEPTH, D_A), f32),
        "w_br_a": nrm(ks[11], (DEPTH, D_A, D_MODEL), f32) * D_A ** -0.5,
        "conv_w": nrm(ks[12], (DEPTH, CONV_W, D_B), f32) * CONV_W ** -0.5,
        "conv_b": 0.01 * nrm(ks[13], (DEPTH, D_B), f32),
        "ln_g": 1.0 + 0.01 * nrm(ks[14], (DEPTH, D_B), f32),
        "ln_b": 0.01 * nrm(ks[15], (DEPTH, D_B), f32),
        "w_br_b": nrm(ks[16], (DEPTH, D_B, D_MODEL), f32) * D_B ** -0.5,
        "w_o": nrm(ks[17], (DEPTH, D_MODEL, D_MODEL), f32) * D_MODEL ** -0.5,
        "w_ple": nrm(ks[18], (DEPTH, PLE_DIM, D_MODEL), f32) * PLE_DIM ** -0.5,
        "w_ple_gate": nrm(ks[19], (DEPTH, D_MODEL, D_MODEL), f32) * D_MODEL ** -0.5,
        "norm_ple": 1.0 + 0.01 * nrm(ks[20], (DEPTH, D_MODEL), f32),
        "norm_final": 1.0 + 0.01 * nrm(ks[21], (D_MODEL,), f32),
    }


def reference(x_prompt, x_sample, state_hgrn, state_conv, p_prompt, p_sample, lb_param, norm_mix,
              w_in, b_in, gnorm_a, w_br_a, conv_w, conv_b, ln_g, ln_b, w_br_b, w_o, w_ple,
              w_ple_gate, norm_ple, norm_final):
    lbs = jnp.cumsum(jax.nn.softmax(lb_param.astype(jnp.float32), axis=0), axis=0)
    lbs = lbs - lbs[0:1]
    bp = x_prompt.shape[0]
    hp, hs = x_prompt, x_sample
    hgrn_p, conv_p, hgrn_s, conv_s = [], [], [], []
    for l in range(DEPTH):
        params = (lbs[l], norm_mix[l], w_in[l], b_in[l], gnorm_a[l], w_br_a[l], conv_w[l], conv_b[l],
                  ln_g[l], ln_b[l], w_br_b[l], w_o[l], w_ple[l], w_ple_gate[l], norm_ple[l])
        s0_p = jnp.zeros((bp, H_A, DK_A, DV_A), jnp.float32)
        buf_p = jnp.zeros((bp, CONV_W - 1, D_B), x_prompt.dtype)
        hp, sp, cp = _layer(hp, p_prompt[l], s0_p, buf_p, *params)
        hs, ss, cs = _layer(hs, p_sample[l], state_hgrn[l], state_conv[l], *params)
        hgrn_p.append(sp)
        conv_p.append(cp)
        hgrn_s.append(ss)
        conv_s.append(cs)
    y_prompt = _rmsnorm(hp, norm_final)
    y_sample = _rmsnorm(hs, norm_final)
    return (y_prompt, y_sample, jnp.stack(hgrn_p), jnp.stack(conv_p), jnp.stack(hgrn_s), jnp.stack(conv_s))
```

```python
import functools

import jax
import jax.numpy as jnp
from jax import lax
from jax.experimental import pallas as pl
from jax.experimental.pallas import tpu as pltpu

F32 = jnp.float32
BF16 = jnp.bfloat16

D_MODEL = 1024
N_HEADS = 8
HEAD = 128
CONV_W = 31
HIST = CONV_W - 1
PLE_DIM = 256
EPS = 1e-6
CHUNK = 64
CHUNK_LEVELS = 6
N_COLS = 9
C_Q, C_F, C_I, C_ZA, C_GA, C_GB, C_ZB, C_MA, C_MB = (i * D_MODEL for i in range(N_COLS))

LANES = 128
SUBLANES = 8
VMEM_LIMIT_BYTES = 60 * 1024 * 1024

N_SLABS = D_MODEL // LANES
HIST_PAD = 32
CONV_ROWS = 64
TT_PROMPT = 256
SAMPLE_T = 8
SAMPLE_LEVELS = 3
SAMPLE_SEQS = 8

NT_DIMS = (((1,), (1,)), ((), ()))
TN_DIMS = (((0,), (0,)), ((), ()))


def _silu(x):
    return x * jax.nn.sigmoid(x)


def _rms(x, g):
    return x * lax.rsqrt(jnp.mean(x * x, axis=-1, keepdims=True) + EPS) * g


def _bcast_row(x, i):
    return jnp.broadcast_to(x[i:i + 1, :], (SUBLANES, x.shape[1]))


def _lower_bound(lb_ref, l):
    if l == 0:
        return jnp.zeros((1, D_MODEL), F32)
    lb = lb_ref[...]
    e = jnp.exp(lb - jnp.max(lb, axis=0, keepdims=True))
    rid = lax.broadcasted_iota(jnp.int32, lb.shape, 0)
    num = jnp.sum(jnp.where((rid >= 1) & (rid <= l), e, 0.0), axis=0, keepdims=True)
    return num / jnp.sum(e, axis=0, keepdims=True)


def _proj(l, col, xn_ref, w_in_ref, b_in_ref):
    return (jnp.dot(xn_ref[...], w_in_ref[:, col:col + D_MODEL], preferred_element_type=F32)
            + b_in_ref[l:l + 1, col:col + D_MODEL])


def _stage_in(l, h, lb_ref, nm_ref, w_in_ref, b_in_ref, xn_ref, q_ref, lf_ref, k_ref, v_ref):
    xn_ref[...] = _rms(h, nm_ref[l:l + 1, :]).astype(BF16)
    proj = functools.partial(_proj, l, xn_ref=xn_ref, w_in_ref=w_in_ref, b_in_ref=b_in_ref)
    q_ref[...] = _silu(proj(C_Q))
    lb = _lower_bound(lb_ref, l)
    f = lb + (1.0 - lb) * jax.nn.sigmoid(proj(C_F))
    lf_ref[...] = jnp.log(jnp.maximum(f, 1e-30))
    k_ref[...] = 1.0 - f
    v_ref[...] = proj(C_I).astype(v_ref.dtype)


def _glu(l, xn_ref, w_in_ref, b_in_ref):
    proj = functools.partial(_proj, l, xn_ref=xn_ref, w_in_ref=w_in_ref, b_in_ref=b_in_ref)
    return proj(C_GA) * jax.nn.sigmoid(proj(C_GB))


def _stage_out(l, final, h, p, o, cv, xn_ref, w_in_ref, b_in_ref, gn_ref, w_bra_ref, cb_ref, lng_ref, lnb_ref,
               w_brb_ref, w_o_ref, w_ple_ref, w_pg_ref, npl_ref, nf_ref):
    proj = functools.partial(_proj, l, xn_ref=xn_ref, w_in_ref=w_in_ref, b_in_ref=b_in_ref)
    gn = gn_ref[l:l + 1, :]
    parts = []
    for hd in range(N_HEADS):
        sl = slice(hd * HEAD, (hd + 1) * HEAD)
        parts.append(_rms(o[:, sl], gn[:, sl]))
    on = jnp.concatenate(parts, axis=1)
    y_a = jnp.dot((on * _silu(proj(C_ZA))).astype(BF16), w_bra_ref[...], preferred_element_type=F32)

    cv = cv + cb_ref[l:l + 1, :]
    xc = cv - jnp.mean(cv, axis=-1, keepdims=True)
    var = jnp.mean(xc * xc, axis=-1, keepdims=True)
    ln = xc * lax.rsqrt(var + EPS) * lng_ref[l:l + 1, :] + lnb_ref[l:l + 1, :]
    y_b = jnp.dot((_silu(ln) * _silu(proj(C_ZB))).astype(BF16), w_brb_ref[...], preferred_element_type=F32)

    merged = jax.nn.sigmoid(proj(C_MA)) * y_a + jax.nn.sigmoid(proj(C_MB)) * y_b
    h = h + jnp.dot(merged.astype(BF16), w_o_ref[...], preferred_element_type=F32)
    gate = jax.nn.sigmoid(jnp.dot(_rms(h, npl_ref[l:l + 1, :]).astype(BF16), w_pg_ref[...],
                                  preferred_element_type=F32))
    h = h + gate * jnp.dot(p.astype(BF16), w_ple_ref[...], preferred_element_type=F32)
    if final:
        h = _rms(h, nf_ref[...])
    return h


def _intra(q, lf, kk, v_bf, levels):
    rows = q.shape[0]
    n_groups = rows // SUBLANES
    groups_per_seg = max((1 << levels) // SUBLANES, 1)
    rid = lax.broadcasted_iota(jnp.int32, (SUBLANES, HEAD), 0)
    grp = lambda x, g: x[g * SUBLANES:(g + 1) * SUBLANES]
    qg = [grp(q, g) for g in range(n_groups)]
    kg = [grp(kk, g) for g in range(n_groups)]

    bg, bl = [], []
    for g in range(n_groups):
        y = grp(lf, g)
        for s in (1, 2, 4):
            y = y + jnp.where(rid >= s, pltpu.roll(y, s, 0), 0.0)
        if g % groups_per_seg:
            y = y + bl[g - 1]
        bg.append(y)
        bl.append(_bcast_row(y, SUBLANES - 1))

    prods = []
    for lvl in range(levels):
        m = 1 << lvl
        xs = []
        for g in range(n_groups):
            if m >= SUBLANES:
                ref_g = (g * SUBLANES // (2 * m) * 2 * m + m) // SUBLANES - 1
                if (g * SUBLANES // m) & 1:
                    xs.append(qg[g] * jnp.exp(bg[g] - bl[ref_g]))
                else:
                    xs.append(kg[g] * jnp.exp(bl[ref_g] - bg[g]))
                continue
            if m == 4:
                ref = _bcast_row(bg[g], 3)
            elif m == 2:
                ref = jnp.where(rid < 4, _bcast_row(bg[g], 1), _bcast_row(bg[g], 5))
            else:
                ref = jnp.where((rid & 1) == 1, pltpu.roll(bg[g], 1, 0), bg[g])
            upper = ((rid >> lvl) & 1) == 1
            dz = bg[g] - ref
            xs.append(jnp.where(upper, qg[g], kg[g]) * jnp.exp(jnp.where(upper, dz, -dz)))
        x = jnp.concatenate(xs, axis=0).astype(BF16)
        prods.append(lax.dot_general(x, x, NT_DIMS, preferred_element_type=F32))

    t = lax.broadcasted_iota(jnp.int32, (rows, rows), 0)
    s = lax.broadcasted_iota(jnp.int32, (rows, rows), 1)
    diff_bits = t ^ s
    a = jnp.where(t == s, jnp.sum(q * kk, axis=1, keepdims=True), 0.0)
    for lvl in range(levels):
        a = jnp.where((t > s) & ((diff_bits >> lvl) == 1), prods[lvl], a)
    o = jnp.dot(a.astype(BF16), v_bf, preferred_element_type=F32)
    return o, bg, bl, qg, kg


def _prompt_kernel(l, final, tt, h_ref, p_ref, lb_ref, nm_ref, w_in_ref, b_in_ref, gn_ref, w_bra_ref, cw_ref,
                   cb_ref, lng_ref, lnb_ref, w_brb_ref, w_o_ref, w_ple_ref, w_pg_ref, npl_ref, nf_ref,
                   ho_ref, so_ref, co_ref,
                   xn_ref, q_ref, lf_ref, k_ref, v_ref, o_ref, st_ref, ubuf_ref, cvs_ref):
    step = pl.program_id(1)

    @pl.when(step == 0)
    def _():
        st_ref[...] = jnp.zeros_like(st_ref)
        ubuf_ref[:, 0:HIST_PAD, :] = jnp.zeros((N_SLABS, HIST_PAD, LANES), F32)

    h = h_ref[...]
    _stage_in(l, h, lb_ref, nm_ref, w_in_ref, b_in_ref, xn_ref, q_ref, lf_ref, k_ref, v_ref)

    def chunk(c, carry):
        rows = pl.ds(pl.multiple_of(c * CHUNK, CHUNK), CHUNK)
        for hd in range(N_HEADS):
            sl = slice(hd * HEAD, (hd + 1) * HEAD)
            v_bf = v_ref[rows, sl]
            o_intra, bg, bl, qg, kg = _intra(q_ref[rows, sl], lf_ref[rows, sl], k_ref[rows, sl], v_bf,
                                            CHUNK_LEVELS)
            b_last = bl[-1]
            qe = jnp.concatenate([qg[g] * jnp.exp(bg[g]) for g in range(len(bg))], axis=0).astype(BF16)
            ke = jnp.concatenate([kg[g] * jnp.exp(b_last - bg[g]) for g in range(len(bg))], axis=0).astype(BF16)
            st = st_ref[hd]
            o_ref[rows, sl] = o_intra + lax.dot_general(qe, st.astype(BF16), NT_DIMS, preferred_element_type=F32)
            st_ref[hd] = (st * jnp.exp(b_last[0:1, :])
                          + lax.dot_general(v_bf, ke, TN_DIMS, preferred_element_type=F32))
        return carry

    lax.fori_loop(0, tt // CHUNK, chunk, 0)

    u = _glu(l, xn_ref, w_in_ref, b_in_ref)
    for c in range(N_SLABS):
        ubuf_ref[c, HIST_PAD:, :] = u[:, c * LANES:(c + 1) * LANES]

    def conv_slab(c, carry):
        for rb in range(tt // CONV_ROWS):
            acc = jnp.zeros((CONV_ROWS, LANES), F32)
            for j in range(CONV_W):
                acc = acc + cw_ref[c, j:j + 1, :] * ubuf_ref[c, pl.ds(rb * CONV_ROWS + j + HIST_PAD - HIST,
                                                                      CONV_ROWS), :]
            cvs_ref[c, rb * CONV_ROWS:(rb + 1) * CONV_ROWS, :] = acc
        return carry

    lax.fori_loop(0, N_SLABS, conv_slab, 0)

    @pl.when(step == pl.num_programs(1) - 1)
    def _():
        for hd in range(N_HEADS):
            so_ref[hd] = st_ref[hd].T
        for c in range(N_SLABS):
            co_ref[:, c * LANES:(c + 1) * LANES] = ubuf_ref[c, tt + HIST_PAD - HIST:tt + HIST_PAD, :]

    ubuf_ref[:, 0:HIST_PAD, :] = ubuf_ref[:, tt:tt + HIST_PAD, :]

    cv = jnp.concatenate([cvs_ref[c] for c in range(N_SLABS)], axis=1)
    ho_ref[...] = _stage_out(l, final, h, p_ref[...], o_ref[...], cv, xn_ref, w_in_ref, b_in_ref, gn_ref,
                             w_bra_ref, cb_ref, lng_ref, lnb_ref, w_brb_ref, w_o_ref, w_ple_ref, w_pg_ref,
                             npl_ref, nf_ref)


def _sample_kernel(l, final, h_ref, p_ref, si_ref, sc_ref, lb_ref, nm_ref, w_in_ref, b_in_ref, gn_ref, w_bra_ref,
                   cw_ref, cb_ref, lng_ref, lnb_ref, w_brb_ref, w_o_ref, w_ple_ref, w_pg_ref, npl_ref, nf_ref,
                   ho_ref, so_ref, co_ref,
                   xn_ref, q_ref, lf_ref, k_ref, v_ref, o_ref, qe_ref, ke_ref, d_ref, u_ref, ubuf_ref, cvs_ref):
    h = h_ref[...]
    _stage_in(l, h, lb_ref, nm_ref, w_in_ref, b_in_ref, xn_ref, q_ref, lf_ref, k_ref, v_ref)

    for hd in range(N_HEADS):
        sl = slice(hd * HEAD, (hd + 1) * HEAD)
        o_intra, bg, bl, qg, kg = _intra(q_ref[:, sl], lf_ref[:, sl], k_ref[:, sl], v_ref[:, sl].astype(BF16),
                                        SAMPLE_LEVELS)
        n = len(bg)
        o_ref[:, sl] = o_intra
        qe_ref[:, sl] = jnp.concatenate([qg[g] * jnp.exp(bg[g]) for g in range(n)], axis=0)
        ke_ref[:, sl] = jnp.concatenate([kg[g] * jnp.exp(bl[g] - bg[g]) for g in range(n)], axis=0)
        d_ref[:, sl] = jnp.concatenate([jnp.exp(bl[g]) for g in range(n)], axis=0)
    u_ref[...] = _glu(l, xn_ref, w_in_ref, b_in_ref)

    def seq(i, carry):
        r0 = pl.multiple_of(i * SAMPLE_T, SAMPLE_T)
        rows = pl.ds(r0, SAMPLE_T)
        for hd in range(N_HEADS):
            sl = slice(hd * HEAD, (hd + 1) * HEAD)
            s = si_ref[i, hd]
            o_ref[rows, sl] = o_ref[rows, sl] + jnp.dot(qe_ref[rows, sl], s, preferred_element_type=F32)
            decay = jnp.broadcast_to(d_ref[pl.ds(r0, 1), sl], (HEAD, HEAD)).T
            so_ref[i, hd] = s * decay + lax.dot_general(ke_ref[rows, sl], v_ref[rows, sl], TN_DIMS,
                                                        preferred_element_type=F32)
        for c in range(N_SLABS):
            cs = slice(c * LANES, (c + 1) * LANES)
            ubuf_ref[c, HIST_PAD - HIST:HIST_PAD, :] = sc_ref[i, :, cs]
            ubuf_ref[c, HIST_PAD:HIST_PAD + SAMPLE_T, :] = u_ref[rows, cs]
            acc = jnp.zeros((SAMPLE_T, LANES), F32)
            for j in range(CONV_W):
                acc = acc + cw_ref[c, j:j + 1, :] * ubuf_ref[c, j + HIST_PAD - HIST:j + HIST_PAD - HIST + SAMPLE_T, :]
            cvs_ref[c, rows, :] = acc
            co_ref[i, :, cs] = ubuf_ref[c, HIST_PAD + SAMPLE_T - HIST:HIST_PAD + SAMPLE_T, :]
        return carry

    lax.fori_loop(0, SAMPLE_SEQS, seq, 0)

    cv = jnp.concatenate([cvs_ref[c] for c in range(N_SLABS)], axis=1)
    ho_ref[...] = _stage_out(l, final, h, p_ref[...], o_ref[...], cv, xn_ref, w_in_ref, b_in_ref, gn_ref,
                             w_bra_ref, cb_ref, lng_ref, lnb_ref, w_brb_ref, w_o_ref, w_ple_ref, w_pg_ref,
                             npl_ref, nf_ref)


def _layer_params(depth, l):
    depth_vec = lambda n: pl.BlockSpec((depth, n), lambda *_: (0, 0))

    def weight(*shape):
        return pl.BlockSpec((None,) + shape, lambda *_: (l,) + (0,) * len(shape), pipeline_mode=pl.Buffered(1))

    return [
        depth_vec(D_MODEL),
        depth_vec(D_MODEL),
        weight(D_MODEL, N_COLS * D_MODEL),
        depth_vec(N_COLS * D_MODEL),
        depth_vec(D_MODEL),
        weight(D_MODEL, D_MODEL),
        weight(N_SLABS, CONV_W, LANES),
        depth_vec(D_MODEL),
        depth_vec(D_MODEL),
        depth_vec(D_MODEL),
        weight(D_MODEL, D_MODEL),
        weight(D_MODEL, D_MODEL),
        weight(PLE_DIM, D_MODEL),
        weight(D_MODEL, D_MODEL),
        depth_vec(D_MODEL),
        pl.BlockSpec((1, D_MODEL), lambda *_: (0, 0)),
    ]


def _prompt_layer(l, final, h, p_all, params):
    depth = p_all.shape[0]
    batch, seq_len, _ = h.shape
    tt = min(TT_PROMPT, seq_len)
    assert seq_len % tt == 0 and tt % CHUNK == 0 and tt >= HIST_PAD and tt % CONV_ROWS == 0
    kern = functools.partial(_prompt_kernel, l, final, tt)
    return pl.pallas_call(
        kern,
        grid=(batch, seq_len // tt),
        in_specs=[pl.BlockSpec((None, tt, D_MODEL), lambda b, t: (b, t, 0)),
                  pl.BlockSpec((None, None, tt, PLE_DIM), lambda b, t: (l, b, t, 0))] + _layer_params(depth, l),
        out_specs=[pl.BlockSpec((None, tt, D_MODEL), lambda b, t: (b, t, 0)),
                   pl.BlockSpec((None, N_HEADS, HEAD, HEAD), lambda b, t: (b, 0, 0, 0)),
                   pl.BlockSpec((None, HIST, D_MODEL), lambda b, t: (b, 0, 0))],
        out_shape=[jax.ShapeDtypeStruct(h.shape, F32),
                   jax.ShapeDtypeStruct((batch, N_HEADS, HEAD, HEAD), F32),
                   jax.ShapeDtypeStruct((batch, HIST, D_MODEL), F32)],
        scratch_shapes=[pltpu.VMEM((tt, D_MODEL), BF16),
                        pltpu.VMEM((tt, D_MODEL), F32),
                        pltpu.VMEM((tt, D_MODEL), F32),
                        pltpu.VMEM((tt, D_MODEL), F32),
                        pltpu.VMEM((tt, D_MODEL), BF16),
                        pltpu.VMEM((tt, D_MODEL), F32),
                        pltpu.VMEM((N_HEADS, HEAD, HEAD), F32),
                        pltpu.VMEM((N_SLABS, HIST_PAD + tt, LANES), F32),
                        pltpu.VMEM((N_SLABS, tt, LANES), F32)],
        compiler_params=pltpu.CompilerParams(dimension_semantics=("arbitrary", "arbitrary"),
                                             vmem_limit_bytes=VMEM_LIMIT_BYTES),
        name=f"prompt_layer{l}",
    )(h, p_all, *params)


def _sample_layer(l, final, h, p_all, state_hgrn, state_conv, params):
    depth = p_all.shape[0]
    n_tok = h.shape[0]
    n_seq = state_hgrn.shape[1]
    tile = SAMPLE_SEQS * SAMPLE_T
    assert n_tok == n_seq * SAMPLE_T and n_seq % SAMPLE_SEQS == 0
    kern = functools.partial(_sample_kernel, l, final)
    act = lambda: pltpu.VMEM((tile, D_MODEL), F32)
    return pl.pallas_call(
        kern,
        grid=(n_seq // SAMPLE_SEQS,),
        in_specs=[pl.BlockSpec((tile, D_MODEL), lambda g: (g, 0)),
                  pl.BlockSpec((None, tile, PLE_DIM), lambda g: (l, g, 0)),
                  pl.BlockSpec((None, SAMPLE_SEQS, N_HEADS, HEAD, HEAD), lambda g: (l, g, 0, 0, 0)),
                  pl.BlockSpec((None, SAMPLE_SEQS, HIST, D_MODEL), lambda g: (l, g, 0, 0))]
        + _layer_params(depth, l),
        out_specs=[pl.BlockSpec((tile, D_MODEL), lambda g: (g, 0)),
                   pl.BlockSpec((SAMPLE_SEQS, N_HEADS, HEAD, HEAD), lambda g: (g, 0, 0, 0)),
                   pl.BlockSpec((SAMPLE_SEQS, HIST, D_MODEL), lambda g: (g, 0, 0))],
        out_shape=[jax.ShapeDtypeStruct(h.shape, F32),
                   jax.ShapeDtypeStruct((n_seq, N_HEADS, HEAD, HEAD), F32),
                   jax.ShapeDtypeStruct((n_seq, HIST, D_MODEL), F32)],
        scratch_shapes=[pltpu.VMEM((tile, D_MODEL), BF16),
                        act(), act(), act(), act(), act(),
                        act(), act(), act(), act(),
                        pltpu.VMEM((N_SLABS, HIST_PAD + SAMPLE_T, LANES), F32),
                        pltpu.VMEM((N_SLABS, tile, LANES), F32)],
        compiler_params=pltpu.CompilerParams(dimension_semantics=("arbitrary",),
                                             vmem_limit_bytes=VMEM_LIMIT_BYTES),
        name=f"sample_layer{l}",
    )(h, p_all, state_hgrn, state_conv, *params)


def kernel(x_prompt, x_sample, state_hgrn, state_conv, p_prompt, p_sample, lb_param, norm_mix, w_in, b_in, gnorm_a,
           w_br_a, conv_w, conv_b, ln_g, ln_b, w_br_b, w_o, w_ple, w_ple_gate, norm_ple, norm_final):
    depth = w_in.shape[0]
    n_seq, sample_t, _ = x_sample.shape
    assert sample_t == SAMPLE_T
    conv_w_slabs = conv_w.reshape(depth, CONV_W, N_SLABS, LANES).transpose(0, 2, 1, 3)
    params = (lb_param, norm_mix, w_in.astype(BF16), b_in, gnorm_a, w_br_a.astype(BF16), conv_w_slabs, conv_b,
              ln_g, ln_b, w_br_b.astype(BF16), w_o.astype(BF16), w_ple.astype(BF16), w_ple_gate.astype(BF16),
              norm_ple, norm_final.reshape(1, D_MODEL))
    hp = x_prompt
    hs = x_sample.reshape(n_seq * SAMPLE_T, D_MODEL)
    ps = p_sample.reshape(depth, n_seq * SAMPLE_T, PLE_DIM)
    hgrn_p, conv_p, hgrn_s, conv_s = [], [], [], []
    for l in range(depth):
        final = l == depth - 1
        hp, sp, cp = _prompt_layer(l, final, hp, p_prompt, params)
        hs, ss, cs = _sample_layer(l, final, hs, ps, state_hgrn, state_conv, params)
        hgrn_p.append(sp)
        conv_p.append(cp)
        hgrn_s.append(ss)
        conv_s.append(cs)
    return (hp, hs.reshape(x_sample.shape), jnp.stack(hgrn_p), jnp.stack(conv_p), jnp.stack(hgrn_s),
            jnp.stack(conv_s))
```

```python
import functools

import numpy as np

import jax
import jax.numpy as jnp
from jax import lax
from jax.experimental import pallas as pl
from jax.experimental.pallas import tpu as pltpu

F32 = jnp.float32
BF16 = jnp.bfloat16

D_MODEL = 1024
N_HEADS = 8
HEAD = 128
CONV_W = 31
HIST = CONV_W - 1
PLE_DIM = 256
EPS = 1e-6
CHUNK = 64
N_COLS = 9
C_Q, C_F, C_I, C_ZA, C_GA, C_GB, C_ZB, C_MA, C_MB = (i * D_MODEL for i in range(N_COLS))

LANES = 128
SUBLANES = 8
VMEM_LIMIT_BYTES = 60 * 1024 * 1024

N_SLABS = D_MODEL // LANES
PITCH = SUBLANES + 1
HIST_PAD = 32
CONV_ROWS = 64
TT_PROMPT = 256
PROJ_PART = 256
SAMPLE_T = 8
SAMPLE_SEQS = 8

NT_DIMS = (((1,), (1,)), ((), ()))
TN_DIMS = (((0,), (0,)), ((), ()))
N_MASKS = 7


def _pair_masks():
    i = np.arange(CHUNK)
    t, s = i[:, None], i[None, :]
    hi, lo = (t >> 3) ^ (s >> 3), (t ^ s) & 7
    masks = [((lo == 0) & (t > s) & ((hi >> j) == 1)) for j in range(3)]
    masks += [((t > s) & ((hi >> j) == 1)) for j in range(3)]
    masks.append(t == s)
    return np.stack(masks).astype(np.float32)


def _silu(x):
    hx = 0.5 * x
    return hx * jnp.tanh(hx) + hx


def _sigmoid(x):
    return 0.5 * jnp.tanh(0.5 * x) + 0.5


def _rms(x, g):
    return x * lax.rsqrt(jnp.mean(x * x, axis=-1, keepdims=True) + EPS) * g


def _lower_bound(lb_ref, l):
    if l == 0:
        return jnp.zeros((1, D_MODEL), F32)
    lb = lb_ref[...]
    e = jnp.exp(lb - jnp.max(lb, axis=0, keepdims=True))
    rid = lax.broadcasted_iota(jnp.int32, lb.shape, 0)
    num = jnp.sum(jnp.where((rid >= 1) & (rid <= l), e, 0.0), axis=0, keepdims=True)
    return num / jnp.sum(e, axis=0, keepdims=True)


def _proj(l, col, xn_ref, w_in_ref, b_in_ref, width=D_MODEL):
    return (jnp.dot(xn_ref[...], w_in_ref[:, col:col + width], preferred_element_type=F32)
            + b_in_ref[l:l + 1, col:col + width])


def _interleave(*thunk_lists):
    n = max(len(t) for t in thunk_lists)
    done = [0] * len(thunk_lists)
    for k in range(n):
        for li, thunks in enumerate(thunk_lists):
            upto = (k + 1) * len(thunks) // n
            while done[li] < upto:
                thunks[done[li]]()
                done[li] += 1


def _to_slabs(ref, x):
    for g in range(x.shape[0] // SUBLANES):
        for n in range(N_SLABS):
            ref[n, g * PITCH:g * PITCH + SUBLANES, :] = x[g * SUBLANES:(g + 1) * SUBLANES, n * LANES:(n + 1) * LANES]


def _from_slabs(ref, n_groups, pitch):
    return jnp.concatenate(
        [jnp.concatenate([ref[n, g * pitch:g * pitch + SUBLANES, :] for g in range(n_groups)], axis=0)
         for n in range(N_SLABS)], axis=1)


def _stage_in(l, h, lb_ref, nm_ref, w_in_ref, b_in_ref, xn_ref, hq_ref, hb_ref, hk_ref, hv_ref):
    xn_ref[...] = _rms(h, nm_ref[l:l + 1, :]).astype(BF16)
    proj = functools.partial(_proj, l, xn_ref=xn_ref, w_in_ref=w_in_ref, b_in_ref=b_in_ref)
    _to_slabs(hq_ref, _silu(proj(C_Q)))
    lb = _lower_bound(lb_ref, l)
    f = lb + (1.0 - lb) * jax.nn.sigmoid(proj(C_F))
    _to_slabs(hb_ref, jnp.log(jnp.maximum(f, 1e-30)))
    _to_slabs(hk_ref, 1.0 - f)
    _to_slabs(hv_ref, proj(C_I))


def _glu(l, xn_ref, w_in_ref, b_in_ref, col=0, width=D_MODEL):
    proj = functools.partial(_proj, l, xn_ref=xn_ref, w_in_ref=w_in_ref, b_in_ref=b_in_ref, width=width)
    return proj(C_GA + col) * _sigmoid(proj(C_GB + col))


GATE_COLS = (C_ZA, C_ZB, C_MA, C_MB)
GATE_ACTS = (_silu, _silu, _sigmoid, _sigmoid)


def _stage_out(l, final, h, p, o, cv, gate, gn_ref, w_bra_ref, cb_ref, lng_ref, lnb_ref,
               w_brb_ref, w_o_ref, w_ple_ref, w_pg_ref, npl_ref, nf_ref):
    gn = gn_ref[l:l + 1, :]
    parts = []
    for hd in range(N_HEADS):
        sl = slice(hd * HEAD, (hd + 1) * HEAD)
        parts.append(_rms(o[:, sl], gn[:, sl]))
    on = jnp.concatenate(parts, axis=1)
    y_a = jnp.dot((on * gate(0)).astype(BF16), w_bra_ref[...], preferred_element_type=F32)

    cv = cv + cb_ref[l:l + 1, :]
    xc = cv - jnp.mean(cv, axis=-1, keepdims=True)
    var = jnp.mean(xc * xc, axis=-1, keepdims=True)
    ln = xc * lax.rsqrt(var + EPS) * lng_ref[l:l + 1, :] + lnb_ref[l:l + 1, :]
    y_b = jnp.dot((_silu(ln) * gate(1)).astype(BF16), w_brb_ref[...], preferred_element_type=F32)

    merged = gate(2) * y_a + gate(3) * y_b
    h = h + jnp.dot(merged.astype(BF16), w_o_ref[...], preferred_element_type=F32)
    gate = _sigmoid(jnp.dot(_rms(h, npl_ref[l:l + 1, :]).astype(BF16), w_pg_ref[...], preferred_element_type=F32))
    h = h + gate * jnp.dot(p.astype(BF16), w_ple_ref[...], preferred_element_type=F32)
    if final:
        h = _rms(h, nf_ref[...])
    return h


def _level_operand(j, qv, kv, bv, ref_of, index_is_row):
    xs = []
    for i in range(SUBLANES):
        ref_i = ((i >> (j + 1)) << (j + 1)) + (1 << j) - 1
        if (i >> j) & 1:
            xs.append(qv[i] * jnp.exp(bv[i] - ref_of(ref_i)))
        elif index_is_row and i == ref_i:
            xs.append(kv[i])
        else:
            xs.append(kv[i] * jnp.exp(ref_of(ref_i) - bv[i]))
    return jnp.concatenate(xs, axis=0).astype(BF16)


def _strided(ref, hd, p0, i):
    return ref[hd, pl.ds(p0 + i, SUBLANES, stride=PITCH), :]


def _natural(ref, hd, p0, a):
    return ref[hd, p0 + a * PITCH:p0 + a * PITCH + SUBLANES, :]


def _hgrn_cumsum(hd, p0, carry_groups, hb_ref):
    rng = range(SUBLANES)
    bs = [_strided(hb_ref, hd, p0, 0)]
    for i in rng[1:]:
        bs.append(bs[-1] + _strided(hb_ref, hd, p0, i))
    if carry_groups:
        rid = lax.broadcasted_iota(jnp.int32, (SUBLANES, HEAD), 0)
        tot = bs[-1]
        inc = tot
        for s in (1, 2, 4):
            inc = inc + jnp.where(rid >= s, pltpu.roll(inc, s, 0), 0.0)
        bs = [x + (inc - tot) for x in bs]
    for i in rng:
        hb_ref[hd, pl.ds(p0 + i, SUBLANES, stride=PITCH), :] = bs[i]


def _hgrn_operands(c, hd, p0, across_groups, hq_ref, hb_ref, hk_ref, x_ref, qk_ref):
    rng = range(SUBLANES)
    qs = [_strided(hq_ref, hd, p0, i) for i in rng]
    ks = [_strided(hk_ref, hd, p0, i) for i in rng]
    bs = [_strided(hb_ref, hd, p0, i) for i in rng]
    for j in range(3):
        x_ref[c, j, hd] = _level_operand(j, qs, ks, bs, lambda i: bs[i], True)
    if not across_groups:
        return
    last = lambda a: hb_ref[hd, p0 + a * PITCH + SUBLANES - 1:p0 + a * PITCH + SUBLANES, :]
    qn = [_natural(hq_ref, hd, p0, a) for a in rng]
    kn = [_natural(hk_ref, hd, p0, a) for a in rng]
    bn = [_natural(hb_ref, hd, p0, a) for a in rng]
    for j in range(3):
        x_ref[c, 3 + j, hd] = _level_operand(j, qn, kn, bn, last, False)
    b_last = last(SUBLANES - 1)
    qk_ref[c, 0, hd] = jnp.concatenate([qn[a] * jnp.exp(bn[a]) for a in rng], axis=0).astype(BF16)
    qk_ref[c, 1, hd] = jnp.concatenate([kn[a] * jnp.exp(b_last - bn[a]) for a in rng], axis=0).astype(BF16)


def _hgrn_scores(c, hd, p0, across_groups, hq_ref, hk_ref, x_ref, m_ref, a_ref):
    rng = range(SUBLANES)
    diag = jnp.concatenate([jnp.sum(_strided(hq_ref, hd, p0, i) * _strided(hk_ref, hd, p0, i), axis=1,
                                    keepdims=True) for i in rng], axis=0)
    a_f = m_ref[N_MASKS - 1] * diag
    for j in range(3):
        x = x_ref[c, j, hd]
        a_f = a_f + lax.dot_general(x, x, NT_DIMS, preferred_element_type=F32) * m_ref[j]
    a_ref[c, 0, hd] = a_f.astype(BF16)
    if not across_groups:
        return
    a_c = None
    for j in range(3):
        x = x_ref[c, 3 + j, hd]
        term = lax.dot_general(x, x, NT_DIMS, preferred_element_type=F32) * m_ref[3 + j]
        a_c = term if a_c is None else a_c + term
    a_ref[c, 1, hd] = a_c.astype(BF16)


def _hgrn_fine_out(c, hd, p0, hv_ref, a_ref, ho_ref):
    rng = range(SUBLANES)
    vs = jnp.concatenate([_strided(hv_ref, hd, p0, i) for i in rng], axis=0).astype(BF16)
    o_f = jnp.dot(a_ref[c, 0, hd], vs, preferred_element_type=F32)
    for i in rng:
        ho_ref[hd, pl.ds(p0 + i, SUBLANES, stride=PITCH), :] = o_f[i * SUBLANES:(i + 1) * SUBLANES]


def _prompt_state_out(c, hd, p0, hb_ref, hv_ref, a_ref, qk_ref, ho_ref, st_ref):
    rng = range(SUBLANES)
    vn = jnp.concatenate([_natural(hv_ref, hd, p0, a) for a in rng], axis=0).astype(BF16)
    b_last = hb_ref[hd, p0 + CHUNK // SUBLANES * PITCH - 2:p0 + CHUNK // SUBLANES * PITCH - 1, :]
    st = st_ref[hd]
    o_n = (jnp.dot(a_ref[c, 1, hd], vn, preferred_element_type=F32)
           + lax.dot_general(qk_ref[c, 0, hd], st.astype(BF16), NT_DIMS, preferred_element_type=F32))
    for a in rng:
        rows = slice(p0 + a * PITCH, p0 + a * PITCH + SUBLANES)
        ho_ref[hd, rows, :] = ho_ref[hd, rows, :] + o_n[a * SUBLANES:(a + 1) * SUBLANES]
    st_ref[hd] = st * jnp.exp(b_last) + lax.dot_general(vn, qk_ref[c, 1, hd], TN_DIMS, preferred_element_type=F32)


def _prompt_kernel(l, final, tt, h_ref, p_ref, m_ref, lb_ref, nm_ref, w_in_ref, b_in_ref, gn_ref, w_bra_ref, cw_ref,
                   cb_ref, lng_ref, lnb_ref, w_brb_ref, w_o_ref, w_ple_ref, w_pg_ref, npl_ref, nf_ref,
                   ho_ref, so_ref, co_ref,
                   xn_ref, hq_ref, hb_ref, hk_ref, hv_ref, hout_ref, x_ref, qk_ref, a_ref, st_ref, ubuf_ref, cvs_ref,
                   gate_ref):
    step = pl.program_id(1)

    @pl.when(step == 0)
    def _():
        st_ref[...] = jnp.zeros_like(st_ref)
        ubuf_ref[:, 0:HIST_PAD, :] = jnp.zeros((N_SLABS, HIST_PAD, LANES), F32)

    h = h_ref[...]
    _stage_in(l, h, lb_ref, nm_ref, w_in_ref, b_in_ref, xn_ref, hq_ref, hb_ref, hk_ref, hv_ref)

    units = [(c, hd, c * (CHUNK // SUBLANES) * PITCH) for c in range(tt // CHUNK) for hd in range(N_HEADS)]
    thunk = functools.partial

    def glu_part(part):
        u = _glu(l, xn_ref, w_in_ref, b_in_ref, part * PROJ_PART, PROJ_PART)
        for n in range(PROJ_PART // LANES):
            ubuf_ref[part * (PROJ_PART // LANES) + n, HIST_PAD:, :] = u[:, n * LANES:(n + 1) * LANES]

    def gate_part(i, part):
        cols = slice(part * PROJ_PART, (part + 1) * PROJ_PART)
        gate_ref[i, :, cols] = GATE_ACTS[i](_proj(l, GATE_COLS[i] + part * PROJ_PART, xn_ref, w_in_ref, b_in_ref,
                                                  PROJ_PART))

    def conv_block(c, rb):
        acc = jnp.zeros((CONV_ROWS, LANES), F32)
        for j in range(CONV_W):
            r0 = rb * CONV_ROWS + j + HIST_PAD - HIST
            acc = acc + cw_ref[c, j:j + 1, :] * ubuf_ref[c, r0:r0 + CONV_ROWS, :]
        cvs_ref[c, rb * CONV_ROWS:(rb + 1) * CONV_ROWS, :] = acc

    n_parts = D_MODEL // PROJ_PART
    _interleave(
        [thunk(_hgrn_cumsum, hd, p0, True, hb_ref) for c, hd, p0 in units]
        + [thunk(_hgrn_operands, c, hd, p0, True, hq_ref, hb_ref, hk_ref, x_ref, qk_ref) for c, hd, p0 in units],
        [thunk(glu_part, part) for part in range(n_parts)]
        + [thunk(gate_part, i, part) for i in range(len(GATE_COLS)) for part in range(n_parts)])
    _interleave(
        [thunk(conv_block, c, rb) for c in range(N_SLABS) for rb in range(tt // CONV_ROWS)],
        [thunk(_hgrn_scores, c, hd, p0, True, hq_ref, hk_ref, x_ref, m_ref, a_ref) for c, hd, p0 in units]
        + [thunk(_hgrn_fine_out, c, hd, p0, hv_ref, a_ref, hout_ref) for c, hd, p0 in units]
        + [thunk(_prompt_state_out, c, hd, p0, hb_ref, hv_ref, a_ref, qk_ref, hout_ref, st_ref)
           for c, hd, p0 in units])

    @pl.when(step == pl.num_programs(1) - 1)
    def _():
        for hd in range(N_HEADS):
            so_ref[hd] = st_ref[hd].T
        for c in range(N_SLABS):
            co_ref[:, c * LANES:(c + 1) * LANES] = ubuf_ref[c, tt + HIST_PAD - HIST:tt + HIST_PAD, :]

    ubuf_ref[:, 0:HIST_PAD, :] = ubuf_ref[:, tt:tt + HIST_PAD, :]

    o = _from_slabs(hout_ref, tt // SUBLANES, PITCH)
    cv = jnp.concatenate([cvs_ref[c] for c in range(N_SLABS)], axis=1)
    ho_ref[...] = _stage_out(l, final, h, p_ref[...], o, cv, lambda i: gate_ref[i], gn_ref, w_bra_ref, cb_ref,
                             lng_ref, lnb_ref, w_brb_ref, w_o_ref, w_ple_ref, w_pg_ref, npl_ref, nf_ref)


def _sample_kernel(l, final, h_ref, p_ref, si_ref, sc_ref, m_ref, lb_ref, nm_ref, w_in_ref, b_in_ref, gn_ref,
                   w_bra_ref, cw_ref, cb_ref, lng_ref, lnb_ref, w_brb_ref, w_o_ref, w_ple_ref, w_pg_ref, npl_ref,
                   nf_ref,
                   ho_ref, so_ref, co_ref,
                   xn_ref, hq_ref, hb_ref, hk_ref, hv_ref, hout_ref, x_ref, a_ref, us_ref, cvs_ref):
    h = h_ref[...]
    _stage_in(l, h, lb_ref, nm_ref, w_in_ref, b_in_ref, xn_ref, hq_ref, hb_ref, hk_ref, hv_ref)

    for hd in range(N_HEADS):
        _hgrn_cumsum(hd, 0, False, hb_ref)
    for hd in range(N_HEADS):
        _hgrn_operands(0, hd, 0, False, hq_ref, hb_ref, hk_ref, x_ref, None)
    for hd in range(N_HEADS):
        _hgrn_scores(0, hd, 0, False, hq_ref, hk_ref, x_ref, m_ref, a_ref)
    for hd in range(N_HEADS):
        _hgrn_fine_out(0, hd, 0, hv_ref, a_ref, hout_ref)

    def seq(i, carry):
        rows = pl.ds(i * PITCH, SAMPLE_T)
        for hd in range(N_HEADS):
            b = hb_ref[hd, rows, :]
            b_last = hb_ref[hd, pl.ds(i * PITCH + SAMPLE_T - 1, 1), :]
            s = si_ref[i, hd]
            qe = hq_ref[hd, rows, :] * jnp.exp(b)
            ke = hk_ref[hd, rows, :] * jnp.exp(b_last - b)
            hout_ref[hd, rows, :] = hout_ref[hd, rows, :] + jnp.dot(qe, s, preferred_element_type=F32)
            decay = jnp.broadcast_to(jnp.exp(b_last), (HEAD, HEAD)).T
            so_ref[i, hd] = s * decay + lax.dot_general(ke, hv_ref[hd, rows, :], TN_DIMS,
                                                        preferred_element_type=F32)
        return carry

    lax.fori_loop(0, SAMPLE_SEQS, seq, 0)

    _to_slabs(us_ref, _glu(l, xn_ref, w_in_ref, b_in_ref))
    for c in range(N_SLABS):
        cs = slice(c * LANES, (c + 1) * LANES)
        ut = [us_ref[c, pl.ds(t, SAMPLE_SEQS, stride=PITCH), :] for t in range(SAMPLE_T)]
        full = [sc_ref[r, :, cs] for r in range(HIST)] + ut
        for t in range(SAMPLE_T):
            acc = jnp.zeros((SAMPLE_SEQS, LANES), F32)
            for j in range(CONV_W):
                acc = acc + cw_ref[c, j:j + 1, :] * full[t + j]
            cvs_ref[c, pl.ds(t, SAMPLE_SEQS, stride=PITCH), :] = acc
        for r in range(HIST):
            co_ref[r, :, cs] = full[r + SAMPLE_T]

    o = _from_slabs(hout_ref, SAMPLE_SEQS, PITCH)
    cv = _from_slabs(cvs_ref, SAMPLE_SEQS, PITCH)
    gate = lambda i: GATE_ACTS[i](_proj(l, GATE_COLS[i], xn_ref, w_in_ref, b_in_ref))
    ho_ref[...] = _stage_out(l, final, h, p_ref[...], o, cv, gate, gn_ref, w_bra_ref, cb_ref,
                             lng_ref, lnb_ref, w_brb_ref, w_o_ref, w_ple_ref, w_pg_ref, npl_ref, nf_ref)


def _layer_params(depth, l):
    depth_vec = lambda n: pl.BlockSpec((depth, n), lambda *_: (0, 0))

    def weight(*shape):
        return pl.BlockSpec((None,) + shape, lambda *_: (l,) + (0,) * len(shape), pipeline_mode=pl.Buffered(1))

    return [
        pl.BlockSpec((N_MASKS, CHUNK, CHUNK), lambda *_: (0, 0, 0)),
        depth_vec(D_MODEL),
        depth_vec(D_MODEL),
        weight(D_MODEL, N_COLS * D_MODEL),
        depth_vec(N_COLS * D_MODEL),
        depth_vec(D_MODEL),
        weight(D_MODEL, D_MODEL),
        weight(N_SLABS, CONV_W, LANES),
        depth_vec(D_MODEL),
        depth_vec(D_MODEL),
        depth_vec(D_MODEL),
        weight(D_MODEL, D_MODEL),
        weight(D_MODEL, D_MODEL),
        weight(PLE_DIM, D_MODEL),
        weight(D_MODEL, D_MODEL),
        depth_vec(D_MODEL),
        pl.BlockSpec((1, D_MODEL), lambda *_: (0, 0)),
    ]


def _head_slabs(rows):
    return pltpu.VMEM((N_HEADS, rows // SUBLANES * PITCH, HEAD), F32)


def _prompt_layer(l, final, h, p_all, params):
    depth = p_all.shape[0]
    batch, seq_len, _ = h.shape
    tt = min(TT_PROMPT, seq_len)
    assert seq_len % tt == 0 and tt % CHUNK == 0 and tt >= HIST_PAD and tt % CONV_ROWS == 0
    kern = functools.partial(_prompt_kernel, l, final, tt)
    return pl.pallas_call(
        kern,
        grid=(batch, seq_len // tt),
        in_specs=[pl.BlockSpec((None, tt, D_MODEL), lambda b, t: (b, t, 0)),
                  pl.BlockSpec((None, None, tt, PLE_DIM), lambda b, t: (l, b, t, 0))] + _layer_params(depth, l),
        out_specs=[pl.BlockSpec((None, tt, D_MODEL), lambda b, t: (b, t, 0)),
                   pl.BlockSpec((None, N_HEADS, HEAD, HEAD), lambda b, t: (b, 0, 0, 0)),
                   pl.BlockSpec((None, HIST, D_MODEL), lambda b, t: (b, 0, 0))],
        out_shape=[jax.ShapeDtypeStruct(h.shape, F32),
                   jax.ShapeDtypeStruct((batch, N_HEADS, HEAD, HEAD), F32),
                   jax.ShapeDtypeStruct((batch, HIST, D_MODEL), F32)],
        scratch_shapes=[pltpu.VMEM((tt, D_MODEL), BF16),
                        _head_slabs(tt), _head_slabs(tt), _head_slabs(tt), _head_slabs(tt),
                        _head_slabs(tt),
                        pltpu.VMEM((tt // CHUNK, 6, N_HEADS, CHUNK, HEAD), BF16),
                        pltpu.VMEM((tt // CHUNK, 2, N_HEADS, CHUNK, HEAD), BF16),
                        pltpu.VMEM((tt // CHUNK, 2, N_HEADS, CHUNK, CHUNK), BF16),
                        pltpu.VMEM((N_HEADS, HEAD, HEAD), F32),
                        pltpu.VMEM((N_SLABS, HIST_PAD + tt, LANES), F32),
                        pltpu.VMEM((N_SLABS, tt, LANES), F32),
                        pltpu.VMEM((len(GATE_COLS), tt, D_MODEL), F32)],
        compiler_params=pltpu.CompilerParams(dimension_semantics=("arbitrary", "arbitrary"),
                                             vmem_limit_bytes=VMEM_LIMIT_BYTES),
        name=f"prompt_layer{l}",
    )(h, p_all, *params)


def _sample_layer(l, final, h, p_all, state_hgrn, state_conv_t, params):
    depth = p_all.shape[0]
    n_tok = h.shape[0]
    n_seq = state_hgrn.shape[1]
    tile = SAMPLE_SEQS * SAMPLE_T
    assert n_tok == n_seq * SAMPLE_T and n_seq % SAMPLE_SEQS == 0 and tile == CHUNK
    kern = functools.partial(_sample_kernel, l, final)
    return pl.pallas_call(
        kern,
        grid=(n_seq // SAMPLE_SEQS,),
        in_specs=[pl.BlockSpec((tile, D_MODEL), lambda g: (g, 0)),
                  pl.BlockSpec((None, tile, PLE_DIM), lambda g: (l, g, 0)),
                  pl.BlockSpec((None, SAMPLE_SEQS, N_HEADS, HEAD, HEAD), lambda g: (l, g, 0, 0, 0)),
                  pl.BlockSpec((None, HIST, SAMPLE_SEQS, D_MODEL), lambda g: (l, 0, g, 0))]
        + _layer_params(depth, l),
        out_specs=[pl.BlockSpec((tile, D_MODEL), lambda g: (g, 0)),
                   pl.BlockSpec((SAMPLE_SEQS, N_HEADS, HEAD, HEAD), lambda g: (g, 0, 0, 0)),
                   pl.BlockSpec((HIST, SAMPLE_SEQS, D_MODEL), lambda g: (0, g, 0))],
        out_shape=[jax.ShapeDtypeStruct(h.shape, F32),
                   jax.ShapeDtypeStruct((n_seq, N_HEADS, HEAD, HEAD), F32),
                   jax.ShapeDtypeStruct((HIST, n_seq, D_MODEL), F32)],
        scratch_shapes=[pltpu.VMEM((tile, D_MODEL), BF16),
                        _head_slabs(tile), _head_slabs(tile), _head_slabs(tile), _head_slabs(tile),
                        _head_slabs(tile),
                        pltpu.VMEM((1, 3, N_HEADS, CHUNK, HEAD), BF16),
                        pltpu.VMEM((1, 1, N_HEADS, CHUNK, CHUNK), BF16),
                        pltpu.VMEM((N_SLABS, SAMPLE_SEQS * PITCH, LANES), F32),
                        pltpu.VMEM((N_SLABS, SAMPLE_SEQS * PITCH, LANES), F32)],
        compiler_params=pltpu.CompilerParams(dimension_semantics=("arbitrary",),
                                             vmem_limit_bytes=VMEM_LIMIT_BYTES),
        name=f"sample_layer{l}",
    )(h, p_all, state_hgrn, state_conv_t, *params)


def kernel(x_prompt, x_sample, state_hgrn, state_conv, p_prompt, p_sample, lb_param, norm_mix, w_in, b_in, gnorm_a,
           w_br_a, conv_w, conv_b, ln_g, ln_b, w_br_b, w_o, w_ple, w_ple_gate, norm_ple, norm_final):
    depth = w_in.shape[0]
    n_seq, sample_t, _ = x_sample.shape
    assert sample_t == SAMPLE_T
    conv_w_slabs = conv_w.reshape(depth, CONV_W, N_SLABS, LANES).transpose(0, 2, 1, 3)
    params = (jnp.asarray(_pair_masks()), lb_param, norm_mix, w_in.astype(BF16), b_in, gnorm_a, w_br_a.astype(BF16),
              conv_w_slabs, conv_b, ln_g, ln_b, w_br_b.astype(BF16), w_o.astype(BF16), w_ple.astype(BF16),
              w_ple_gate.astype(BF16), norm_ple, norm_final.reshape(1, D_MODEL))
    hp = x_prompt
    hs = x_sample.reshape(n_seq * SAMPLE_T, D_MODEL)
    ps = p_sample.reshape(depth, n_seq * SAMPLE_T, PLE_DIM)
    sc_t = state_conv.transpose(0, 2, 1, 3)
    hgrn_p, conv_p, hgrn_s, conv_s = [], [], [], []
    for l in range(depth):
        final = l == depth - 1
        hp, sp, cp = _prompt_layer(l, final, hp, p_prompt, params)
        hs, ss, cs = _sample_layer(l, final, hs, ps, state_hgrn, sc_t, params)
        hgrn_p.append(sp)
        conv_p.append(cp)
        hgrn_s.append(ss)
        conv_s.append(cs)
    return (hp, hs.reshape(x_sample.shape), jnp.stack(hgrn_p), jnp.stack(conv_p), jnp.stack(hgrn_s),
            jnp.stack(conv_s).transpose(0, 2, 1, 3))
```

```python
import functools

import numpy as np

import jax
import jax.numpy as jnp
from jax import lax
from jax.experimental import pallas as pl
from jax.experimental.pallas import tpu as pltpu

F32 = jnp.float32
BF16 = jnp.bfloat16

D_MODEL = 1024
N_HEADS = 8
HEAD = 128
CONV_W = 31
HIST = CONV_W - 1
PLE_DIM = 256
EPS = 1e-6
CHUNK = 64
N_COLS = 9
C_Q, C_F, C_I, C_ZA, C_GA, C_GB, C_ZB, C_MA, C_MB = (i * D_MODEL for i in range(N_COLS))

LANES = 128
SUBLANES = 8
VMEM_LIMIT_BYTES = 60 * 1024 * 1024

N_SLABS = D_MODEL // LANES
PITCH = SUBLANES + 1
HIST_PAD = 32
CONV_ROWS = 64
TT_PROMPT = 256
PROJ_PART = 256
SAMPLE_T = 8
SAMPLE_SEQS = 8

NT_DIMS = (((1,), (1,)), ((), ()))
TN_DIMS = (((0,), (0,)), ((), ()))
N_MASKS = 7


def _pair_masks():
    i = np.arange(CHUNK)
    t, s = i[:, None], i[None, :]
    hi, lo = (t >> 3) ^ (s >> 3), (t ^ s) & 7
    masks = [((lo == 0) & (t > s) & ((hi >> j) == 1)) for j in range(3)]
    masks += [((t > s) & ((hi >> j) == 1)) for j in range(3)]
    masks.append(t == s)
    return np.stack(masks).astype(np.float32)


def _silu(x):
    hx = 0.5 * x
    return hx * jnp.tanh(hx) + hx


def _sigmoid(x):
    return 0.5 * jnp.tanh(0.5 * x) + 0.5


def _rms(x, g):
    return x * lax.rsqrt(jnp.mean(x * x, axis=-1, keepdims=True) + EPS) * g


def _row(ref, l, cols=slice(None)):
    return ref[pl.ds(l, 1), cols]


def _lower_bound(lb_ref, l):
    lb = lb_ref[...]
    e = jnp.exp(lb - jnp.max(lb, axis=0, keepdims=True))
    rid = lax.broadcasted_iota(jnp.int32, lb.shape, 0)
    num = jnp.sum(jnp.where((rid >= 1) & (rid <= l), e, 0.0), axis=0, keepdims=True)
    return num / jnp.sum(e, axis=0, keepdims=True)


def _proj(l, col, xn_ref, w_in_ref, b_in_ref, width=D_MODEL):
    return (jnp.dot(xn_ref[...], w_in_ref[:, col:col + width], preferred_element_type=F32)
            + _row(b_in_ref, l, slice(col, col + width)))


def _interleave(*thunk_lists):
    n = max(len(t) for t in thunk_lists)
    done = [0] * len(thunk_lists)
    for k in range(n):
        for li, thunks in enumerate(thunk_lists):
            upto = (k + 1) * len(thunks) // n
            while done[li] < upto:
                thunks[done[li]]()
                done[li] += 1


def _to_slabs(ref, x):
    for g in range(x.shape[0] // SUBLANES):
        for n in range(N_SLABS):
            ref[n, g * PITCH:g * PITCH + SUBLANES, :] = x[g * SUBLANES:(g + 1) * SUBLANES, n * LANES:(n + 1) * LANES]


def _from_slabs(ref, n_groups, pitch):
    return jnp.concatenate(
        [jnp.concatenate([ref[n, g * pitch:g * pitch + SUBLANES, :] for g in range(n_groups)], axis=0)
         for n in range(N_SLABS)], axis=1)


def _stage_in(l, h, lb_ref, nm_ref, w_in_ref, b_in_ref, xn_ref, hq_ref, hb_ref, hk_ref, hv_ref):
    xn_ref[...] = _rms(h, _row(nm_ref, l)).astype(BF16)
    proj = functools.partial(_proj, l, xn_ref=xn_ref, w_in_ref=w_in_ref, b_in_ref=b_in_ref)
    _to_slabs(hq_ref, _silu(proj(C_Q)))
    lb = _lower_bound(lb_ref, l)
    f = lb + (1.0 - lb) * jax.nn.sigmoid(proj(C_F))
    _to_slabs(hb_ref, jnp.log(jnp.maximum(f, 1e-30)))
    _to_slabs(hk_ref, 1.0 - f)
    _to_slabs(hv_ref, proj(C_I))


def _glu(l, xn_ref, w_in_ref, b_in_ref, col=0, width=D_MODEL):
    proj = functools.partial(_proj, l, xn_ref=xn_ref, w_in_ref=w_in_ref, b_in_ref=b_in_ref, width=width)
    return proj(C_GA + col) * _sigmoid(proj(C_GB + col))


GATE_COLS = (C_ZA, C_ZB, C_MA, C_MB)
GATE_ACTS = (_silu, _silu, _sigmoid, _sigmoid)


def _stage_out(l, final, h, p, o, cv, gate, gn_ref, w_bra_ref, cb_ref, lng_ref, lnb_ref,
               w_brb_ref, w_o_ref, w_ple_ref, w_pg_ref, npl_ref, nf_ref):
    gn = _row(gn_ref, l)
    parts = []
    for hd in range(N_HEADS):
        sl = slice(hd * HEAD, (hd + 1) * HEAD)
        parts.append(_rms(o[:, sl], gn[:, sl]))
    on = jnp.concatenate(parts, axis=1)
    y_a = jnp.dot((on * gate(0)).astype(BF16), w_bra_ref[...], preferred_element_type=F32)

    cv = cv + _row(cb_ref, l)
    xc = cv - jnp.mean(cv, axis=-1, keepdims=True)
    var = jnp.mean(xc * xc, axis=-1, keepdims=True)
    ln = xc * lax.rsqrt(var + EPS) * _row(lng_ref, l) + _row(lnb_ref, l)
    y_b = jnp.dot((_silu(ln) * gate(1)).astype(BF16), w_brb_ref[...], preferred_element_type=F32)

    merged = gate(2) * y_a + gate(3) * y_b
    h = h + jnp.dot(merged.astype(BF16), w_o_ref[...], preferred_element_type=F32)
    gate = _sigmoid(jnp.dot(_rms(h, _row(npl_ref, l)).astype(BF16), w_pg_ref[...], preferred_element_type=F32))
    h = h + gate * jnp.dot(p.astype(BF16), w_ple_ref[...], preferred_element_type=F32)
    if final:
        h = _rms(h, nf_ref[...])
    return h


def _level_operand(j, qv, kv, bv, ref_of, index_is_row):
    xs = []
    for i in range(SUBLANES):
        ref_i = ((i >> (j + 1)) << (j + 1)) + (1 << j) - 1
        if (i >> j) & 1:
            xs.append(qv[i] * jnp.exp(bv[i] - ref_of(ref_i)))
        elif index_is_row and i == ref_i:
            xs.append(kv[i])
        else:
            xs.append(kv[i] * jnp.exp(ref_of(ref_i) - bv[i]))
    return jnp.concatenate(xs, axis=0).astype(BF16)


def _strided(ref, hd, p0, i):
    return ref[hd, pl.ds(p0 + i, SUBLANES, stride=PITCH), :]


def _natural(ref, hd, p0, a):
    return ref[hd, p0 + a * PITCH:p0 + a * PITCH + SUBLANES, :]


def _hgrn_cumsum(hd, p0, carry_groups, hb_ref):
    rng = range(SUBLANES)
    bs = [_strided(hb_ref, hd, p0, 0)]
    for i in rng[1:]:
        bs.append(bs[-1] + _strided(hb_ref, hd, p0, i))
    if carry_groups:
        rid = lax.broadcasted_iota(jnp.int32, (SUBLANES, HEAD), 0)
        tot = bs[-1]
        inc = tot
        for s in (1, 2, 4):
            inc = inc + jnp.where(rid >= s, pltpu.roll(inc, s, 0), 0.0)
        bs = [x + (inc - tot) for x in bs]
    for i in rng:
        hb_ref[hd, pl.ds(p0 + i, SUBLANES, stride=PITCH), :] = bs[i]


def _hgrn_operands(c, hd, p0, across_groups, hq_ref, hb_ref, hk_ref, x_ref, qk_ref):
    rng = range(SUBLANES)
    qs = [_strided(hq_ref, hd, p0, i) for i in rng]
    ks = [_strided(hk_ref, hd, p0, i) for i in rng]
    bs = [_strided(hb_ref, hd, p0, i) for i in rng]
    for j in range(3):
        x_ref[c, j, hd] = _level_operand(j, qs, ks, bs, lambda i: bs[i], True)
    if not across_groups:
        return
    last = lambda a: hb_ref[hd, p0 + a * PITCH + SUBLANES - 1:p0 + a * PITCH + SUBLANES, :]
    qn = [_natural(hq_ref, hd, p0, a) for a in rng]
    kn = [_natural(hk_ref, hd, p0, a) for a in rng]
    bn = [_natural(hb_ref, hd, p0, a) for a in rng]
    for j in range(3):
        x_ref[c, 3 + j, hd] = _level_operand(j, qn, kn, bn, last, False)
    b_last = last(SUBLANES - 1)
    qk_ref[c, 0, hd] = jnp.concatenate([qn[a] * jnp.exp(bn[a]) for a in rng], axis=0).astype(BF16)
    qk_ref[c, 1, hd] = jnp.concatenate([kn[a] * jnp.exp(b_last - bn[a]) for a in rng], axis=0).astype(BF16)


def _hgrn_scores(c, hd, p0, across_groups, hq_ref, hk_ref, x_ref, m_ref, a_ref):
    rng = range(SUBLANES)
    diag = jnp.concatenate([jnp.sum(_strided(hq_ref, hd, p0, i) * _strided(hk_ref, hd, p0, i), axis=1,
                                    keepdims=True) for i in rng], axis=0)
    a_f = m_ref[N_MASKS - 1] * diag
    for j in range(3):
        x = x_ref[c, j, hd]
        a_f = a_f + lax.dot_general(x, x, NT_DIMS, preferred_element_type=F32) * m_ref[j]
    a_ref[c, 0, hd] = a_f.astype(BF16)
    if not across_groups:
        return
    a_c = None
    for j in range(3):
        x = x_ref[c, 3 + j, hd]
        term = lax.dot_general(x, x, NT_DIMS, preferred_element_type=F32) * m_ref[3 + j]
        a_c = term if a_c is None else a_c + term
    a_ref[c, 1, hd] = a_c.astype(BF16)


def _hgrn_fine_out(c, hd, p0, hv_ref, a_ref, ho_ref):
    rng = range(SUBLANES)
    vs = jnp.concatenate([_strided(hv_ref, hd, p0, i) for i in rng], axis=0).astype(BF16)
    o_f = jnp.dot(a_ref[c, 0, hd], vs, preferred_element_type=F32)
    for i in rng:
        ho_ref[hd, pl.ds(p0 + i, SUBLANES, stride=PITCH), :] = o_f[i * SUBLANES:(i + 1) * SUBLANES]


def _prompt_state_out(c, hd, p0, hb_ref, hv_ref, a_ref, qk_ref, ho_ref, st_ref):
    rng = range(SUBLANES)
    vn = jnp.concatenate([_natural(hv_ref, hd, p0, a) for a in rng], axis=0).astype(BF16)
    b_last = hb_ref[hd, p0 + CHUNK // SUBLANES * PITCH - 2:p0 + CHUNK // SUBLANES * PITCH - 1, :]
    st = st_ref[hd]
    o_n = (jnp.dot(a_ref[c, 1, hd], vn, preferred_element_type=F32)
           + lax.dot_general(qk_ref[c, 0, hd], st.astype(BF16), NT_DIMS, preferred_element_type=F32))
    for a in rng:
        rows = slice(p0 + a * PITCH, p0 + a * PITCH + SUBLANES)
        ho_ref[hd, rows, :] = ho_ref[hd, rows, :] + o_n[a * SUBLANES:(a + 1) * SUBLANES]
    st_ref[hd] = st * jnp.exp(b_last) + lax.dot_general(vn, qk_ref[c, 1, hd], TN_DIMS, preferred_element_type=F32)


def _prompt_kernel(l, final, tt, h_ref, p_ref, m_ref, lb_ref, nm_ref, w_in_ref, b_in_ref, gn_ref, w_bra_ref, cw_ref,
                   cb_ref, lng_ref, lnb_ref, w_brb_ref, w_o_ref, w_ple_ref, w_pg_ref, npl_ref, nf_ref,
                   ho_ref, so_ref, co_ref,
                   xn_ref, hq_ref, hb_ref, hk_ref, hv_ref, hout_ref, x_ref, qk_ref, a_ref, st_ref, ubuf_ref, cvs_ref,
                   gate_ref):
    step = pl.program_id(1)

    @pl.when(step == 0)
    def _():
        st_ref[...] = jnp.zeros_like(st_ref)
        ubuf_ref[:, 0:HIST_PAD, :] = jnp.zeros((N_SLABS, HIST_PAD, LANES), F32)

    h = h_ref[...]
    _stage_in(l, h, lb_ref, nm_ref, w_in_ref, b_in_ref, xn_ref, hq_ref, hb_ref, hk_ref, hv_ref)

    units = [(c, hd, c * (CHUNK // SUBLANES) * PITCH) for c in range(tt // CHUNK) for hd in range(N_HEADS)]
    thunk = functools.partial

    def glu_part(part):
        u = _glu(l, xn_ref, w_in_ref, b_in_ref, part * PROJ_PART, PROJ_PART)
        for n in range(PROJ_PART // LANES):
            ubuf_ref[part * (PROJ_PART // LANES) + n, HIST_PAD:, :] = u[:, n * LANES:(n + 1) * LANES]

    def gate_part(i, part):
        cols = slice(part * PROJ_PART, (part + 1) * PROJ_PART)
        gate_ref[i, :, cols] = GATE_ACTS[i](_proj(l, GATE_COLS[i] + part * PROJ_PART, xn_ref, w_in_ref, b_in_ref,
                                                  PROJ_PART))

    def conv_block(c, rb):
        acc = jnp.zeros((CONV_ROWS, LANES), F32)
        for j in range(CONV_W):
            r0 = rb * CONV_ROWS + j + HIST_PAD - HIST
            acc = acc + cw_ref[c, j:j + 1, :] * ubuf_ref[c, r0:r0 + CONV_ROWS, :]
        cvs_ref[c, rb * CONV_ROWS:(rb + 1) * CONV_ROWS, :] = acc

    n_parts = D_MODEL // PROJ_PART
    _interleave(
        [thunk(_hgrn_cumsum, hd, p0, True, hb_ref) for c, hd, p0 in units]
        + [thunk(_hgrn_operands, c, hd, p0, True, hq_ref, hb_ref, hk_ref, x_ref, qk_ref) for c, hd, p0 in units],
        [thunk(glu_part, part) for part in range(n_parts)]
        + [thunk(gate_part, i, part) for i in range(len(GATE_COLS)) for part in range(n_parts)])
    _interleave(
        [thunk(conv_block, c, rb) for c in range(N_SLABS) for rb in range(tt // CONV_ROWS)],
        [thunk(_hgrn_scores, c, hd, p0, True, hq_ref, hk_ref, x_ref, m_ref, a_ref) for c, hd, p0 in units]
        + [thunk(_hgrn_fine_out, c, hd, p0, hv_ref, a_ref, hout_ref) for c, hd, p0 in units]
        + [thunk(_prompt_state_out, c, hd, p0, hb_ref, hv_ref, a_ref, qk_ref, hout_ref, st_ref)
           for c, hd, p0 in units])

    @pl.when(step == pl.num_programs(1) - 1)
    def _():
        for hd in range(N_HEADS):
            so_ref[hd] = st_ref[hd].T
        for c in range(N_SLABS):
            co_ref[:, c * LANES:(c + 1) * LANES] = ubuf_ref[c, tt + HIST_PAD - HIST:tt + HIST_PAD, :]

    ubuf_ref[:, 0:HIST_PAD, :] = ubuf_ref[:, tt:tt + HIST_PAD, :]

    o = _from_slabs(hout_ref, tt // SUBLANES, PITCH)
    cv = jnp.concatenate([cvs_ref[c] for c in range(N_SLABS)], axis=1)
    ho_ref[...] = _stage_out(l, final, h, p_ref[...], o, cv, lambda i: gate_ref[i], gn_ref, w_bra_ref, cb_ref,
                             lng_ref, lnb_ref, w_brb_ref, w_o_ref, w_ple_ref, w_pg_ref, npl_ref, nf_ref)


def _sample_kernel(h_ref, p_ref, si_ref, sc_ref, m_ref, lb_ref, nm_ref, w_in_ref, b_in_ref, gn_ref,
                   w_bra_ref, cw_ref, cb_ref, lng_ref, lnb_ref, w_brb_ref, w_o_ref, w_ple_ref, w_pg_ref, npl_ref,
                   nf_ref,
                   ho_ref, so_ref, co_ref,
                   hs_ref, xn_ref, hq_ref, hb_ref, hk_ref, hv_ref, hout_ref, x_ref, a_ref, us_ref, cvs_ref):
    l = pl.program_id(0)
    g = pl.program_id(1)

    @pl.when(l == 0)
    def _():
        hs_ref[g] = h_ref[...]

    h = hs_ref[g]
    _stage_in(l, h, lb_ref, nm_ref, w_in_ref, b_in_ref, xn_ref, hq_ref, hb_ref, hk_ref, hv_ref)

    for hd in range(N_HEADS):
        _hgrn_cumsum(hd, 0, False, hb_ref)
    for hd in range(N_HEADS):
        _hgrn_operands(0, hd, 0, False, hq_ref, hb_ref, hk_ref, x_ref, None)
    for hd in range(N_HEADS):
        _hgrn_scores(0, hd, 0, False, hq_ref, hk_ref, x_ref, m_ref, a_ref)
    for hd in range(N_HEADS):
        _hgrn_fine_out(0, hd, 0, hv_ref, a_ref, hout_ref)

    def seq(i, carry):
        rows = pl.ds(i * PITCH, SAMPLE_T)
        for hd in range(N_HEADS):
            b = hb_ref[hd, rows, :]
            b_last = hb_ref[hd, pl.ds(i * PITCH + SAMPLE_T - 1, 1), :]
            s = si_ref[i, hd]
            qe = hq_ref[hd, rows, :] * jnp.exp(b)
            ke = hk_ref[hd, rows, :] * jnp.exp(b_last - b)
            hout_ref[hd, rows, :] = hout_ref[hd, rows, :] + jnp.dot(qe, s, preferred_element_type=F32)
            decay = jnp.broadcast_to(jnp.exp(b_last), (HEAD, HEAD)).T
            so_ref[i, hd] = s * decay + lax.dot_general(ke, hv_ref[hd, rows, :], TN_DIMS,
                                                        preferred_element_type=F32)
        return carry

    lax.fori_loop(0, SAMPLE_SEQS, seq, 0)

    _to_slabs(us_ref, _glu(l, xn_ref, w_in_ref, b_in_ref))
    for c in range(N_SLABS):
        cs = slice(c * LANES, (c + 1) * LANES)
        ut = [us_ref[c, pl.ds(t, SAMPLE_SEQS, stride=PITCH), :] for t in range(SAMPLE_T)]
        full = [sc_ref[r, :, cs] for r in range(HIST)] + ut
        for t in range(SAMPLE_T):
            acc = jnp.zeros((SAMPLE_SEQS, LANES), F32)
            for j in range(CONV_W):
                acc = acc + cw_ref[c, j:j + 1, :] * full[t + j]
            cvs_ref[c, pl.ds(t, SAMPLE_SEQS, stride=PITCH), :] = acc
        for r in range(HIST):
            co_ref[r, :, cs] = full[r + SAMPLE_T]

    o = _from_slabs(hout_ref, SAMPLE_SEQS, PITCH)
    cv = _from_slabs(cvs_ref, SAMPLE_SEQS, PITCH)
    gate = lambda i: GATE_ACTS[i](_proj(l, GATE_COLS[i], xn_ref, w_in_ref, b_in_ref))
    h = _stage_out(l, False, h, p_ref[...], o, cv, gate, gn_ref, w_bra_ref, cb_ref,
                   lng_ref, lnb_ref, w_brb_ref, w_o_ref, w_ple_ref, w_pg_ref, npl_ref, nf_ref)
    hs_ref[g] = h
    ho_ref[...] = _rms(h, nf_ref[...])


def _layer_params(depth, layer_of):
    depth_vec = lambda n: pl.BlockSpec((depth, n), lambda *_: (0, 0))

    def weight(*shape):
        return pl.BlockSpec((None,) + shape, lambda *idx: (layer_of(*idx),) + (0,) * len(shape),
                            pipeline_mode=pl.Buffered(1))

    return [
        pl.BlockSpec((N_MASKS, CHUNK, CHUNK), lambda *_: (0, 0, 0)),
        depth_vec(D_MODEL),
        depth_vec(D_MODEL),
        weight(D_MODEL, N_COLS * D_MODEL),
        depth_vec(N_COLS * D_MODEL),
        depth_vec(D_MODEL),
        weight(D_MODEL, D_MODEL),
        weight(N_SLABS, CONV_W, LANES),
        depth_vec(D_MODEL),
        depth_vec(D_MODEL),
        depth_vec(D_MODEL),
        weight(D_MODEL, D_MODEL),
        weight(D_MODEL, D_MODEL),
        weight(PLE_DIM, D_MODEL),
        weight(D_MODEL, D_MODEL),
        depth_vec(D_MODEL),
        pl.BlockSpec((1, D_MODEL), lambda *_: (0, 0)),
    ]


def _head_slabs(rows):
    return pltpu.VMEM((N_HEADS, rows // SUBLANES * PITCH, HEAD), F32)


def _prompt_layer(l, final, h, p_all, params):
    depth = p_all.shape[0]
    batch, seq_len, _ = h.shape
    tt = min(TT_PROMPT, seq_len)
    assert seq_len % tt == 0 and tt % CHUNK == 0 and tt >= HIST_PAD and tt % CONV_ROWS == 0
    kern = functools.partial(_prompt_kernel, l, final, tt)
    return pl.pallas_call(
        kern,
        grid=(batch, seq_len // tt),
        in_specs=[pl.BlockSpec((None, tt, D_MODEL), lambda b, t: (b, t, 0)),
                  pl.BlockSpec((None, None, tt, PLE_DIM), lambda b, t: (l, b, t, 0))]
        + _layer_params(depth, lambda b, t: l),
        out_specs=[pl.BlockSpec((None, tt, D_MODEL), lambda b, t: (b, t, 0)),
                   pl.BlockSpec((None, N_HEADS, HEAD, HEAD), lambda b, t: (b, 0, 0, 0)),
                   pl.BlockSpec((None, HIST, D_MODEL), lambda b, t: (b, 0, 0))],
        out_shape=[jax.ShapeDtypeStruct(h.shape, F32),
                   jax.ShapeDtypeStruct((batch, N_HEADS, HEAD, HEAD), F32),
                   jax.ShapeDtypeStruct((batch, HIST, D_MODEL), F32)],
        scratch_shapes=[pltpu.VMEM((tt, D_MODEL), BF16),
                        _head_slabs(tt), _head_slabs(tt), _head_slabs(tt), _head_slabs(tt),
                        _head_slabs(tt),
                        pltpu.VMEM((tt // CHUNK, 6, N_HEADS, CHUNK, HEAD), BF16),
                        pltpu.VMEM((tt // CHUNK, 2, N_HEADS, CHUNK, HEAD), BF16),
                        pltpu.VMEM((tt // CHUNK, 2, N_HEADS, CHUNK, CHUNK), BF16),
                        pltpu.VMEM((N_HEADS, HEAD, HEAD), F32),
                        pltpu.VMEM((N_SLABS, HIST_PAD + tt, LANES), F32),
                        pltpu.VMEM((N_SLABS, tt, LANES), F32),
                        pltpu.VMEM((len(GATE_COLS), tt, D_MODEL), F32)],
        compiler_params=pltpu.CompilerParams(dimension_semantics=("arbitrary", "arbitrary"),
                                             vmem_limit_bytes=VMEM_LIMIT_BYTES),
        name=f"prompt_layer{l}",
    )(h, p_all, *params)


def _sample_layers(h, p_all, state_hgrn, state_conv_t, params):
    depth = p_all.shape[0]
    n_tok = h.shape[0]
    n_seq = state_hgrn.shape[1]
    tile = SAMPLE_SEQS * SAMPLE_T
    n_groups = n_seq // SAMPLE_SEQS
    assert n_tok == n_seq * SAMPLE_T and n_seq % SAMPLE_SEQS == 0 and tile == CHUNK
    return pl.pallas_call(
        _sample_kernel,
        grid=(depth, n_groups),
        in_specs=[pl.BlockSpec((tile, D_MODEL), lambda l, g: (g, 0)),
                  pl.BlockSpec((None, tile, PLE_DIM), lambda l, g: (l, g, 0)),
                  pl.BlockSpec((None, SAMPLE_SEQS, N_HEADS, HEAD, HEAD), lambda l, g: (l, g, 0, 0, 0)),
                  pl.BlockSpec((None, HIST, SAMPLE_SEQS, D_MODEL), lambda l, g: (l, 0, g, 0))]
        + _layer_params(depth, lambda l, g: l),
        out_specs=[pl.BlockSpec((None, tile, D_MODEL), lambda l, g: (l, g, 0)),
                   pl.BlockSpec((None, SAMPLE_SEQS, N_HEADS, HEAD, HEAD), lambda l, g: (l, g, 0, 0, 0)),
                   pl.BlockSpec((None, HIST, SAMPLE_SEQS, D_MODEL), lambda l, g: (l, 0, g, 0))],
        out_shape=[jax.ShapeDtypeStruct((depth,) + h.shape, F32),
                   jax.ShapeDtypeStruct((depth, n_seq, N_HEADS, HEAD, HEAD), F32),
                   jax.ShapeDtypeStruct((depth, HIST, n_seq, D_MODEL), F32)],
        scratch_shapes=[pltpu.VMEM((n_groups, tile, D_MODEL), F32),
                        pltpu.VMEM((tile, D_MODEL), BF16),
                        _head_slabs(tile), _head_slabs(tile), _head_slabs(tile), _head_slabs(tile),
                        _head_slabs(tile),
                        pltpu.VMEM((1, 3, N_HEADS, CHUNK, HEAD), BF16),
                        pltpu.VMEM((1, 1, N_HEADS, CHUNK, CHUNK), BF16),
                        pltpu.VMEM((N_SLABS, SAMPLE_SEQS * PITCH, LANES), F32),
                        pltpu.VMEM((N_SLABS, SAMPLE_SEQS * PITCH, LANES), F32)],
        compiler_params=pltpu.CompilerParams(dimension_semantics=("arbitrary", "arbitrary"),
                                             vmem_limit_bytes=VMEM_LIMIT_BYTES),
        name="sample_layers",
    )(h, p_all, state_hgrn, state_conv_t, *params)


def kernel(x_prompt, x_sample, state_hgrn, state_conv, p_prompt, p_sample, lb_param, norm_mix, w_in, b_in, gnorm_a,
           w_br_a, conv_w, conv_b, ln_g, ln_b, w_br_b, w_o, w_ple, w_ple_gate, norm_ple, norm_final):
    depth = w_in.shape[0]
    n_seq, sample_t, _ = x_sample.shape
    assert sample_t == SAMPLE_T
    conv_w_slabs = conv_w.reshape(depth, CONV_W, N_SLABS, LANES).transpose(0, 2, 1, 3)
    params = (jnp.asarray(_pair_masks()), lb_param, norm_mix, w_in.astype(BF16), b_in, gnorm_a, w_br_a.astype(BF16),
              conv_w_slabs, conv_b, ln_g, ln_b, w_br_b.astype(BF16), w_o.astype(BF16), w_ple.astype(BF16),
              w_ple_gate.astype(BF16), norm_ple, norm_final.reshape(1, D_MODEL))
    hp = x_prompt
    hgrn_p, conv_p = [], []
    for l in range(depth):
        hp, sp, cp = _prompt_layer(l, l == depth - 1, hp, p_prompt, params)
        hgrn_p.append(sp)
        conv_p.append(cp)
    ys, hgrn_s, conv_s = _sample_layers(x_sample.reshape(n_seq * SAMPLE_T, D_MODEL),
                                        p_sample.reshape(depth, n_seq * SAMPLE_T, PLE_DIM), state_hgrn,
                                        state_conv.transpose(0, 2, 1, 3), params)
    return (hp, ys[depth - 1].reshape(x_sample.shape), jnp.stack(hgrn_p), jnp.stack(conv_p), hgrn_s,
            conv_s.transpose(0, 2, 1, 3))
```

```python
import functools

import numpy as np

import jax
import jax.numpy as jnp
from jax import lax
from jax.experimental import pallas as pl
from jax.experimental.pallas import tpu as pltpu

F32 = jnp.float32
BF16 = jnp.bfloat16

D_MODEL = 1024
N_HEADS = 8
HEAD = 128
CONV_W = 31
HIST = CONV_W - 1
PLE_DIM = 256
EPS = 1e-6
CHUNK = 64
N_COLS = 9
C_Q, C_F, C_I, C_ZA, C_GA, C_GB, C_ZB, C_MA, C_MB = (i * D_MODEL for i in range(N_COLS))

LANES = 128
SUBLANES = 8
VMEM_LIMIT_BYTES = 60 * 1024 * 1024

N_SLABS = D_MODEL // LANES
PITCH = SUBLANES + 1
HIST_PAD = 32
CONV_ROWS = 64
TT_PROMPT = 256
PROJ_PART = 256
SAMPLE_T = 8
SAMPLE_SEQS = 8

NT_DIMS = (((1,), (1,)), ((), ()))
TN_DIMS = (((0,), (0,)), ((), ()))
N_MASKS = 7


def _pair_masks():
    i = np.arange(CHUNK)
    t, s = i[:, None], i[None, :]
    hi, lo = (t >> 3) ^ (s >> 3), (t ^ s) & 7
    masks = [((lo == 0) & (t > s) & ((hi >> j) == 1)) for j in range(3)]
    masks += [((t > s) & ((hi >> j) == 1)) for j in range(3)]
    masks.append(t == s)
    return np.stack(masks).astype(np.float32)


def _silu(x):
    hx = 0.5 * x
    return hx * jnp.tanh(hx) + hx


def _sigmoid(x):
    return 0.5 * jnp.tanh(0.5 * x) + 0.5


def _rms(x, g):
    return x * lax.rsqrt(jnp.mean(x * x, axis=-1, keepdims=True) + EPS) * g


def _row(ref, l, cols=slice(None)):
    return ref[pl.ds(l, 1), cols]


def _lower_bound(lb_ref, l):
    lb = lb_ref[...]
    e = jnp.exp(lb - jnp.max(lb, axis=0, keepdims=True))
    rid = lax.broadcasted_iota(jnp.int32, lb.shape, 0)
    num = jnp.sum(jnp.where((rid >= 1) & (rid <= l), e, 0.0), axis=0, keepdims=True)
    return num / jnp.sum(e, axis=0, keepdims=True)


def _proj(l, col, xn, w_in_ref, b_in_ref, width=D_MODEL):
    return (jnp.dot(xn, w_in_ref[:, col:col + width], preferred_element_type=F32)
            + _row(b_in_ref, l, slice(col, col + width)))


def _interleave(*thunk_lists):
    n = max(len(t) for t in thunk_lists)
    done = [0] * len(thunk_lists)
    for k in range(n):
        for li, thunks in enumerate(thunk_lists):
            upto = (k + 1) * len(thunks) // n
            while done[li] < upto:
                thunks[done[li]]()
                done[li] += 1


def _to_slabs(ref, x):
    for g in range(x.shape[0] // SUBLANES):
        for n in range(N_SLABS):
            ref[n, g * PITCH:g * PITCH + SUBLANES, :] = x[g * SUBLANES:(g + 1) * SUBLANES, n * LANES:(n + 1) * LANES]


def _from_slabs(ref, n_groups, pitch):
    return jnp.concatenate(
        [jnp.concatenate([ref[n, g * pitch:g * pitch + SUBLANES, :] for g in range(n_groups)], axis=0)
         for n in range(N_SLABS)], axis=1)


def _stage_in(l, h, lb_ref, nm_ref, w_in_ref, b_in_ref, hq_ref, hb_ref, hk_ref, hv_ref):
    xn = _rms(h, _row(nm_ref, l)).astype(BF16)
    proj = functools.partial(_proj, l, xn=xn, w_in_ref=w_in_ref, b_in_ref=b_in_ref)
    _to_slabs(hq_ref, _silu(proj(C_Q)))
    lb = _lower_bound(lb_ref, l)
    f = lb + (1.0 - lb) * jax.nn.sigmoid(proj(C_F))
    _to_slabs(hb_ref, jnp.log(jnp.maximum(f, 1e-30)))
    _to_slabs(hk_ref, 1.0 - f)
    _to_slabs(hv_ref, proj(C_I))
    return xn


def _glu(l, xn, w_in_ref, b_in_ref, col=0, width=D_MODEL):
    proj = functools.partial(_proj, l, xn=xn, w_in_ref=w_in_ref, b_in_ref=b_in_ref, width=width)
    return proj(C_GA + col) * _sigmoid(proj(C_GB + col))


GATE_COLS = (C_ZA, C_ZB, C_MA, C_MB)
GATE_ACTS = (_silu, _silu, _sigmoid, _sigmoid)


def _stage_out(l, final, h, p, o, cv, gate, gn_ref, w_bra_ref, cb_ref, lng_ref, lnb_ref,
               w_brb_ref, w_o_ref, w_ple_ref, w_pg_ref, npl_ref, nf_ref):
    gn = _row(gn_ref, l)
    parts = []
    for hd in range(N_HEADS):
        sl = slice(hd * HEAD, (hd + 1) * HEAD)
        parts.append(_rms(o[:, sl], gn[:, sl]))
    g0 = gate(0)
    lhs = jnp.concatenate([(parts[hd] * g0[:, hd * HEAD:(hd + 1) * HEAD]).astype(BF16) for hd in range(N_HEADS)],
                          axis=1)
    y_a = jnp.dot(lhs, w_bra_ref[...], preferred_element_type=F32)

    cv = cv + _row(cb_ref, l)
    xc = cv - jnp.mean(cv, axis=-1, keepdims=True)
    var = jnp.mean(xc * xc, axis=-1, keepdims=True)
    ln = xc * lax.rsqrt(var + EPS) * _row(lng_ref, l) + _row(lnb_ref, l)
    y_b = jnp.dot((_silu(ln) * gate(1)).astype(BF16), w_brb_ref[...], preferred_element_type=F32)

    merged = gate(2) * y_a + gate(3) * y_b
    h = h + jnp.dot(merged.astype(BF16), w_o_ref[...], preferred_element_type=F32)
    gate = _sigmoid(jnp.dot(_rms(h, _row(npl_ref, l)).astype(BF16), w_pg_ref[...], preferred_element_type=F32))
    h = h + gate * jnp.dot(p.astype(BF16), w_ple_ref[...], preferred_element_type=F32)
    if final:
        h = _rms(h, nf_ref[...])
    return h


def _level_operand(j, qv, kv, bv, ref_of, index_is_row):
    xs = []
    for i in range(SUBLANES):
        ref_i = ((i >> (j + 1)) << (j + 1)) + (1 << j) - 1
        if (i >> j) & 1:
            xs.append(qv[i] * jnp.exp(bv[i] - ref_of(ref_i)))
        elif index_is_row and i == ref_i:
            xs.append(kv[i])
        else:
            xs.append(kv[i] * jnp.exp(ref_of(ref_i) - bv[i]))
    return jnp.concatenate(xs, axis=0).astype(BF16)


def _strided(ref, hd, p0, i):
    return ref[hd, pl.ds(p0 + i, SUBLANES, stride=PITCH), :]


def _natural(ref, hd, p0, a):
    return ref[hd, p0 + a * PITCH:p0 + a * PITCH + SUBLANES, :]


def _hgrn_cumsum(hd, p0, carry_groups, hb_ref):
    rng = range(SUBLANES)
    bs = [_strided(hb_ref, hd, p0, 0)]
    for i in rng[1:]:
        bs.append(bs[-1] + _strided(hb_ref, hd, p0, i))
    if carry_groups:
        rid = lax.broadcasted_iota(jnp.int32, (SUBLANES, HEAD), 0)
        tot = bs[-1]
        inc = tot
        for s in (1, 2, 4):
            inc = inc + jnp.where(rid >= s, pltpu.roll(inc, s, 0), 0.0)
        bs = [x + (inc - tot) for x in bs]
    for i in rng:
        hb_ref[hd, pl.ds(p0 + i, SUBLANES, stride=PITCH), :] = bs[i]


def _hgrn_operands(c, hd, p0, across_groups, hq_ref, hb_ref, hk_ref, x_ref, qk_ref):
    rng = range(SUBLANES)
    qs = [_strided(hq_ref, hd, p0, i) for i in rng]
    ks = [_strided(hk_ref, hd, p0, i) for i in rng]
    bs = [_strided(hb_ref, hd, p0, i) for i in rng]
    for j in range(3):
        x_ref[c, j, hd] = _level_operand(j, qs, ks, bs, lambda i: bs[i], True)
    if not across_groups:
        return
    last = lambda a: hb_ref[hd, p0 + a * PITCH + SUBLANES - 1:p0 + a * PITCH + SUBLANES, :]
    qn = [_natural(hq_ref, hd, p0, a) for a in rng]
    kn = [_natural(hk_ref, hd, p0, a) for a in rng]
    bn = [_natural(hb_ref, hd, p0, a) for a in rng]
    for j in range(3):
        x_ref[c, 3 + j, hd] = _level_operand(j, qn, kn, bn, last, False)
    b_last = last(SUBLANES - 1)
    qk_ref[c, 0, hd] = jnp.concatenate([qn[a] * jnp.exp(bn[a]) for a in rng], axis=0).astype(BF16)
    qk_ref[c, 1, hd] = jnp.concatenate([kn[a] * jnp.exp(b_last - bn[a]) for a in rng], axis=0).astype(BF16)


def _hgrn_scores(c, hd, p0, across_groups, hq_ref, hk_ref, x_ref, m_ref, a_ref):
    rng = range(SUBLANES)
    diag = jnp.concatenate([jnp.sum(_strided(hq_ref, hd, p0, i) * _strided(hk_ref, hd, p0, i), axis=1,
                                    keepdims=True) for i in rng], axis=0)
    a_f = m_ref[N_MASKS - 1] * diag
    for j in range(3):
        x = x_ref[c, j, hd]
        a_f = a_f + lax.dot_general(x, x, NT_DIMS, preferred_element_type=F32) * m_ref[j]
    a_ref[c, 0, hd] = a_f.astype(BF16)
    if not across_groups:
        return
    a_c = None
    for j in range(3):
        x = x_ref[c, 3 + j, hd]
        term = lax.dot_general(x, x, NT_DIMS, preferred_element_type=F32) * m_ref[3 + j]
        a_c = term if a_c is None else a_c + term
    a_ref[c, 1, hd] = a_c.astype(BF16)


def _hgrn_fine_out(c, hd, p0, hv_ref, a_ref, ho_ref):
    rng = range(SUBLANES)
    vs = jnp.concatenate([_strided(hv_ref, hd, p0, i) for i in rng], axis=0).astype(BF16)
    o_f = jnp.dot(a_ref[c, 0, hd], vs, preferred_element_type=F32)
    for i in rng:
        ho_ref[hd, pl.ds(p0 + i, SUBLANES, stride=PITCH), :] = o_f[i * SUBLANES:(i + 1) * SUBLANES]


def _prompt_state_out(c, hd, p0, hb_ref, hv_ref, a_ref, qk_ref, ho_ref, st_ref):
    rng = range(SUBLANES)
    vn = jnp.concatenate([_natural(hv_ref, hd, p0, a) for a in rng], axis=0).astype(BF16)
    b_last = hb_ref[hd, p0 + CHUNK // SUBLANES * PITCH - 2:p0 + CHUNK // SUBLANES * PITCH - 1, :]
    st = st_ref[hd]
    o_n = (jnp.dot(a_ref[c, 1, hd], vn, preferred_element_type=F32)
           + lax.dot_general(qk_ref[c, 0, hd], st.astype(BF16), NT_DIMS, preferred_element_type=F32))
    for a in rng:
        rows = slice(p0 + a * PITCH, p0 + a * PITCH + SUBLANES)
        ho_ref[hd, rows, :] = ho_ref[hd, rows, :] + o_n[a * SUBLANES:(a + 1) * SUBLANES]
    st_ref[hd] = st * jnp.exp(b_last) + lax.dot_general(vn, qk_ref[c, 1, hd], TN_DIMS, preferred_element_type=F32)


def _prompt_kernel(l, final, tt, h_ref, p_ref, m_ref, lb_ref, nm_ref, w_in_ref, b_in_ref, gn_ref, w_bra_ref, cw_ref,
                   cb_ref, lng_ref, lnb_ref, w_brb_ref, w_o_ref, w_ple_ref, w_pg_ref, npl_ref, nf_ref,
                   ho_ref, so_ref, co_ref,
                   hq_ref, hb_ref, hk_ref, hv_ref, hout_ref, x_ref, qk_ref, a_ref, st_ref, ubuf_ref, cvs_ref,
                   gate_ref):
    step = pl.program_id(1)

    @pl.when(step == 0)
    def _():
        st_ref[...] = jnp.zeros_like(st_ref)
        ubuf_ref[:, 0:HIST_PAD, :] = jnp.zeros((N_SLABS, HIST_PAD, LANES), F32)

    h = h_ref[...]
    xn = _stage_in(l, h, lb_ref, nm_ref, w_in_ref, b_in_ref, hq_ref, hb_ref, hk_ref, hv_ref)

    units = [(c, hd, c * (CHUNK // SUBLANES) * PITCH) for c in range(tt // CHUNK) for hd in range(N_HEADS)]
    thunk = functools.partial

    def glu_part(part):
        u = _glu(l, xn, w_in_ref, b_in_ref, part * PROJ_PART, PROJ_PART)
        for n in range(PROJ_PART // LANES):
            ubuf_ref[part * (PROJ_PART // LANES) + n, HIST_PAD:, :] = u[:, n * LANES:(n + 1) * LANES]

    def gate_part(i, part):
        cols = slice(part * PROJ_PART, (part + 1) * PROJ_PART)
        gate_ref[i, :, cols] = GATE_ACTS[i](_proj(l, GATE_COLS[i] + part * PROJ_PART, xn, w_in_ref, b_in_ref,
                                                  PROJ_PART))

    def conv_block(c, rb):
        acc = jnp.zeros((CONV_ROWS, LANES), F32)
        for j in range(CONV_W):
            r0 = rb * CONV_ROWS + j + HIST_PAD - HIST
            acc = acc + cw_ref[c, j:j + 1, :] * ubuf_ref[c, r0:r0 + CONV_ROWS, :]
        cvs_ref[c, rb * CONV_ROWS:(rb + 1) * CONV_ROWS, :] = acc

    n_parts = D_MODEL // PROJ_PART
    _interleave(
        [thunk(_hgrn_cumsum, hd, p0, True, hb_ref) for c, hd, p0 in units]
        + [thunk(_hgrn_operands, c, hd, p0, True, hq_ref, hb_ref, hk_ref, x_ref, qk_ref) for c, hd, p0 in units],
        [thunk(glu_part, part) for part in range(n_parts)]
        + [thunk(gate_part, i, part) for i in range(len(GATE_COLS)) for part in range(n_parts)])
    _interleave(
        [thunk(conv_block, c, rb) for c in range(N_SLABS) for rb in range(tt // CONV_ROWS)],
        [thunk(_hgrn_scores, c, hd, p0, True, hq_ref, hk_ref, x_ref, m_ref, a_ref) for c, hd, p0 in units]
        + [thunk(_hgrn_fine_out, c, hd, p0, hv_ref, a_ref, hout_ref) for c, hd, p0 in units]
        + [thunk(_prompt_state_out, c, hd, p0, hb_ref, hv_ref, a_ref, qk_ref, hout_ref, st_ref)
           for c, hd, p0 in units])

    @pl.when(step == pl.num_programs(1) - 1)
    def _():
        for hd in range(N_HEADS):
            so_ref[hd] = st_ref[hd].T
        for c in range(N_SLABS):
            co_ref[:, c * LANES:(c + 1) * LANES] = ubuf_ref[c, tt + HIST_PAD - HIST:tt + HIST_PAD, :]

    ubuf_ref[:, 0:HIST_PAD, :] = ubuf_ref[:, tt:tt + HIST_PAD, :]

    o = _from_slabs(hout_ref, tt // SUBLANES, PITCH)
    cv = jnp.concatenate([cvs_ref[c] for c in range(N_SLABS)], axis=1)
    ho_ref[...] = _stage_out(l, final, h, p_ref[...], o, cv, lambda i: gate_ref[i], gn_ref, w_bra_ref, cb_ref,
                             lng_ref, lnb_ref, w_brb_ref, w_o_ref, w_ple_ref, w_pg_ref, npl_ref, nf_ref)


def _sample_kernel(h_ref, p_ref, si_ref, sc_ref, m_ref, lb_ref, nm_ref, w_in_ref, b_in_ref, gn_ref,
                   w_bra_ref, cw_ref, cb_ref, lng_ref, lnb_ref, w_brb_ref, w_o_ref, w_ple_ref, w_pg_ref, npl_ref,
                   nf_ref,
                   ho_ref, so_ref, co_ref,
                   hs_ref, hq_ref, hb_ref, hk_ref, hv_ref, hout_ref, x_ref, a_ref, us_ref, cvs_ref):
    l = pl.program_id(0)
    g = pl.program_id(1)

    @pl.when(l == 0)
    def _():
        hs_ref[g] = h_ref[...]

    h = hs_ref[g]
    xn = _stage_in(l, h, lb_ref, nm_ref, w_in_ref, b_in_ref, hq_ref, hb_ref, hk_ref, hv_ref)

    for hd in range(N_HEADS):
        _hgrn_cumsum(hd, 0, False, hb_ref)
    for hd in range(N_HEADS):
        _hgrn_operands(0, hd, 0, False, hq_ref, hb_ref, hk_ref, x_ref, None)
    for hd in range(N_HEADS):
        _hgrn_scores(0, hd, 0, False, hq_ref, hk_ref, x_ref, m_ref, a_ref)
    for hd in range(N_HEADS):
        _hgrn_fine_out(0, hd, 0, hv_ref, a_ref, hout_ref)

    rid = lax.broadcasted_iota(jnp.int32, (SAMPLE_T, HEAD), 0)
    ones_rows = jnp.where(rid < 2, 1.0, 0.0)
    zeros = jnp.zeros((SAMPLE_T, HEAD), F32)

    def seq(i, carry):
        rows = pl.ds(i * PITCH, SAMPLE_T)
        for hd in range(N_HEADS):
            b = hb_ref[hd, rows, :]
            b_last = hb_ref[hd, pl.ds(i * PITCH + SAMPLE_T - 1, 1), :]
            s = si_ref[i, hd]
            qe = hq_ref[hd, rows, :] * jnp.exp(b)
            ke = hk_ref[hd, rows, :] * jnp.exp(b_last - b)
            hout_ref[hd, rows, :] = hout_ref[hd, rows, :] + jnp.dot(qe, s, preferred_element_type=F32)
            d = jnp.exp(b_last)
            d_hi = d.astype(BF16).astype(F32)
            d_rows = jnp.where(rid == 0, d_hi, jnp.where(rid == 1, d - d_hi, 0.0))
            lhs = jnp.concatenate([ke, d_rows], axis=0)
            rhs = jnp.concatenate([jnp.concatenate([hv_ref[hd, rows, :], zeros], axis=1),
                                   jnp.concatenate([zeros, ones_rows], axis=1)], axis=0)
            upd = lax.dot_general(lhs, rhs, TN_DIMS, preferred_element_type=F32)
            so_ref[i, hd] = s * upd[:, HEAD:] + upd[:, :HEAD]
        return carry

    lax.fori_loop(0, SAMPLE_SEQS, seq, 0)

    _to_slabs(us_ref, _glu(l, xn, w_in_ref, b_in_ref))
    for c in range(N_SLABS):
        cs = slice(c * LANES, (c + 1) * LANES)
        ut = [us_ref[c, pl.ds(t, SAMPLE_SEQS, stride=PITCH), :] for t in range(SAMPLE_T)]
        full = [sc_ref[r, :, cs] for r in range(HIST)] + ut
        for t in range(SAMPLE_T):
            acc = jnp.zeros((SAMPLE_SEQS, LANES), F32)
            for j in range(CONV_W):
                acc = acc + cw_ref[c, j:j + 1, :] * full[t + j]
            cvs_ref[c, pl.ds(t, SAMPLE_SEQS, stride=PITCH), :] = acc
        for r in range(HIST):
            co_ref[r, :, cs] = full[r + SAMPLE_T]

    o = _from_slabs(hout_ref, SAMPLE_SEQS, PITCH)
    cv = _from_slabs(cvs_ref, SAMPLE_SEQS, PITCH)
    gate = lambda i: GATE_ACTS[i](_proj(l, GATE_COLS[i], xn, w_in_ref, b_in_ref))
    h = _stage_out(l, False, h, p_ref[...], o, cv, gate, gn_ref, w_bra_ref, cb_ref,
                   lng_ref, lnb_ref, w_brb_ref, w_o_ref, w_ple_ref, w_pg_ref, npl_ref, nf_ref)
    hs_ref[g] = h
    ho_ref[...] = _rms(h, nf_ref[...])


def _layer_params(depth, layer_of):
    depth_vec = lambda n: pl.BlockSpec((depth, n), lambda *_: (0, 0))

    def weight(*shape):
        return pl.BlockSpec((None,) + shape, lambda *idx: (layer_of(*idx),) + (0,) * len(shape),
                            pipeline_mode=pl.Buffered(1))

    return [
        pl.BlockSpec((N_MASKS, CHUNK, CHUNK), lambda *_: (0, 0, 0)),
        depth_vec(D_MODEL),
        depth_vec(D_MODEL),
        weight(D_MODEL, N_COLS * D_MODEL),
        depth_vec(N_COLS * D_MODEL),
        depth_vec(D_MODEL),
        weight(D_MODEL, D_MODEL),
        weight(N_SLABS, CONV_W, LANES),
        depth_vec(D_MODEL),
        depth_vec(D_MODEL),
        depth_vec(D_MODEL),
        weight(D_MODEL, D_MODEL),
        weight(D_MODEL, D_MODEL),
        weight(PLE_DIM, D_MODEL),
        weight(D_MODEL, D_MODEL),
        depth_vec(D_MODEL),
        pl.BlockSpec((1, D_MODEL), lambda *_: (0, 0)),
    ]


def _head_slabs(rows):
    return pltpu.VMEM((N_HEADS, rows // SUBLANES * PITCH, HEAD), F32)


def _prompt_layer(l, final, h, p_all, params):
    depth = p_all.shape[0]
    batch, seq_len, _ = h.shape
    tt = min(TT_PROMPT, seq_len)
    assert seq_len % tt == 0 and tt % CHUNK == 0 and tt >= HIST_PAD and tt % CONV_ROWS == 0
    kern = functools.partial(_prompt_kernel, l, final, tt)
    return pl.pallas_call(
        kern,
        grid=(batch, seq_len // tt),
        in_specs=[pl.BlockSpec((None, tt, D_MODEL), lambda b, t: (b, t, 0)),
                  pl.BlockSpec((None, None, tt, PLE_DIM), lambda b, t: (l, b, t, 0))]
        + _layer_params(depth, lambda b, t: l),
        out_specs=[pl.BlockSpec((None, tt, D_MODEL), lambda b, t: (b, t, 0)),
                   pl.BlockSpec((None, N_HEADS, HEAD, HEAD), lambda b, t: (b, 0, 0, 0)),
                   pl.BlockSpec((None, HIST, D_MODEL), lambda b, t: (b, 0, 0))],
        out_shape=[jax.ShapeDtypeStruct(h.shape, F32),
                   jax.ShapeDtypeStruct((batch, N_HEADS, HEAD, HEAD), F32),
                   jax.ShapeDtypeStruct((batch, HIST, D_MODEL), F32)],
        scratch_shapes=[_head_slabs(tt), _head_slabs(tt), _head_slabs(tt), _head_slabs(tt),
                        _head_slabs(tt),
                        pltpu.VMEM((tt // CHUNK, 6, N_HEADS, CHUNK, HEAD), BF16),
                        pltpu.VMEM((tt // CHUNK, 2, N_HEADS, CHUNK, HEAD), BF16),
                        pltpu.VMEM((tt // CHUNK, 2, N_HEADS, CHUNK, CHUNK), BF16),
                        pltpu.VMEM((N_HEADS, HEAD, HEAD), F32),
                        pltpu.VMEM((N_SLABS, HIST_PAD + tt, LANES), F32),
                        pltpu.VMEM((N_SLABS, tt, LANES), F32),
                        pltpu.VMEM((len(GATE_COLS), tt, D_MODEL), F32)],
        compiler_params=pltpu.CompilerParams(dimension_semantics=("arbitrary", "arbitrary"),
                                             vmem_limit_bytes=VMEM_LIMIT_BYTES),
        name=f"prompt_layer{l}",
    )(h, p_all, *params)


def _sample_layers(h, p_all, state_hgrn, state_conv_t, params):
    depth = p_all.shape[0]
    n_tok = h.shape[0]
    n_seq = state_hgrn.shape[1]
    tile = SAMPLE_SEQS * SAMPLE_T
    n_groups = n_seq // SAMPLE_SEQS
    assert n_tok == n_seq * SAMPLE_T and n_seq % SAMPLE_SEQS == 0 and tile == CHUNK
    return pl.pallas_call(
        _sample_kernel,
        grid=(depth, n_groups),
        in_specs=[pl.BlockSpec((tile, D_MODEL), lambda l, g: (g, 0)),
                  pl.BlockSpec((None, tile, PLE_DIM), lambda l, g: (l, g, 0)),
                  pl.BlockSpec((None, SAMPLE_SEQS, N_HEADS, HEAD, HEAD), lambda l, g: (l, g, 0, 0, 0)),
                  pl.BlockSpec((None, HIST, SAMPLE_SEQS, D_MODEL), lambda l, g: (l, 0, g, 0))]
        + _layer_params(depth, lambda l, g: l),
        out_specs=[pl.BlockSpec((None, tile, D_MODEL), lambda l, g: (l, g, 0)),
                   pl.BlockSpec((None, SAMPLE_SEQS, N_HEADS, HEAD, HEAD), lambda l, g: (l, g, 0, 0, 0)),
                   pl.BlockSpec((None, HIST, SAMPLE_SEQS, D_MODEL), lambda l, g: (l, 0, g, 0))],
        out_shape=[jax.ShapeDtypeStruct((depth,) + h.shape, F32),
                   jax.ShapeDtypeStruct((depth, n_seq, N_HEADS, HEAD, HEAD), F32),
                   jax.ShapeDtypeStruct((depth, HIST, n_seq, D_MODEL), F32)],
        scratch_shapes=[pltpu.VMEM((n_groups, tile, D_MODEL), F32),
                        _head_slabs(tile), _head_slabs(tile), _head_slabs(tile), _head_slabs(tile),
                        _head_slabs(tile),
                        pltpu.VMEM((1, 3, N_HEADS, CHUNK, HEAD), BF16),
                        pltpu.VMEM((1, 1, N_HEADS, CHUNK, CHUNK), BF16),
                        pltpu.VMEM((N_SLABS, SAMPLE_SEQS * PITCH, LANES), F32),
                        pltpu.VMEM((N_SLABS, SAMPLE_SEQS * PITCH, LANES), F32)],
        compiler_params=pltpu.CompilerParams(dimension_semantics=("arbitrary", "arbitrary"),
                                             vmem_limit_bytes=VMEM_LIMIT_BYTES),
        name="sample_layers",
    )(h, p_all, state_hgrn, state_conv_t, *params)


def kernel(x_prompt, x_sample, state_hgrn, state_conv, p_prompt, p_sample, lb_param, norm_mix, w_in, b_in, gnorm_a,
           w_br_a, conv_w, conv_b, ln_g, ln_b, w_br_b, w_o, w_ple, w_ple_gate, norm_ple, norm_final):
    depth = w_in.shape[0]
    n_seq, sample_t, _ = x_sample.shape
    assert sample_t == SAMPLE_T
    conv_w_slabs = conv_w.reshape(depth, CONV_W, N_SLABS, LANES).transpose(0, 2, 1, 3)
    params = (jnp.asarray(_pair_masks()), lb_param, norm_mix, w_in.astype(BF16), b_in, gnorm_a, w_br_a.astype(BF16),
              conv_w_slabs, conv_b, ln_g, ln_b, w_br_b.astype(BF16), w_o.astype(BF16), w_ple.astype(BF16),
              w_ple_gate.astype(BF16), norm_ple, norm_final.reshape(1, D_MODEL))
    hp = x_prompt
    hgrn_p, conv_p = [], []
    for l in range(depth):
        hp, sp, cp = _prompt_layer(l, l == depth - 1, hp, p_prompt, params)
        hgrn_p.append(sp)
        conv_p.append(cp)
    ys, hgrn_s, conv_s = _sample_layers(x_sample.reshape(n_seq * SAMPLE_T, D_MODEL),
                                        p_sample.reshape(depth, n_seq * SAMPLE_T, PLE_DIM), state_hgrn,
                                        state_conv.transpose(0, 2, 1, 3), params)
    return (hp, ys[depth - 1].reshape(x_sample.shape), jnp.stack(hgrn_p), jnp.stack(conv_p), hgrn_s,
            conv_s.transpose(0, 2, 1, 3))
```
